```python
import math
import jax, jax.numpy as jnp
from jax import lax
import numpy as np

D_MODEL = 4096
BATCH = 4
SEQ = 2048
DEPTH = 4
DEC_BATCH = 128
DEC_SEQ = 1
PAST_LEN = 8192
PAGE_SIZE = 128

N_MIXERS = 2
N_SSD = (DEPTH + 1) // 2
N_MLA = DEPTH // 2
EPS = 1e-6

N_MEM = 256
X_HEADS = 4
X_HEAD_DIM = 768
X_WIDTH = X_HEADS * X_HEAD_DIM
X_SCALE = X_HEAD_DIM ** -0.5

SSD_D_INNER = 2 * D_MODEL
SSD_HEAD_DIM = 64
SSD_HEADS = SSD_D_INNER // SSD_HEAD_DIM
SSD_GROUPS = 8
SSD_STATE = 128
SSD_CONV = 4
SSD_CONV_DIM = SSD_D_INNER + 2 * SSD_GROUPS * SSD_STATE
SSD_CHUNK = 128
SSD_MIX_COLS = SSD_D_INNER + SSD_CONV_DIM + SSD_HEADS

MLA_HEADS = 32
MLA_Q_LORA = 1024
MLA_KV_LORA = 512
MLA_NOPE = 128
MLA_ROPE = 64
MLA_V = 128
MLA_WIDTH = MLA_HEADS * MLA_V
MLA_MIX_COLS = MLA_Q_LORA + MLA_KV_LORA + MLA_ROPE + MLA_WIDTH
MLA_SCALE = (MLA_NOPE + MLA_ROPE) ** -0.5
ROPE_THETA = 10000.0
Q_BLOCK = 128

kernel_name = 'hybrid_ssd_mla_memory_decoder_step'


def rmsnorm(x, w):
    xf = x.astype(jnp.float32)
    y = xf * lax.rsqrt(jnp.mean(xf * xf, axis=-1, keepdims=True) + EPS)
    return (y * w.astype(jnp.float32)).astype(x.dtype)


def gated_group_rmsnorm(y, z, w):
    g = (y * jax.nn.silu(z)).astype(jnp.float32)
    shp = g.shape
    g = g.reshape(shp[:-1] + (SSD_GROUPS, -1))
    g = g * lax.rsqrt(jnp.mean(g * g, axis=-1, keepdims=True) + EPS)
    return (g.reshape(shp) * w.astype(jnp.float32)).astype(y.dtype)


def rope(x, pos):
    half = x.shape[-1] // 2
    inv = ROPE_THETA ** (-jnp.arange(half, dtype=jnp.float32) / half)
    ang = pos.astype(jnp.float32)[:, None] * inv[None, :]
    cos = jnp.cos(ang)[None, :, None, :]
    sin = jnp.sin(ang)[None, :, None, :]
    xf = x.astype(jnp.float32)
    x1, x2 = xf[..., :half], xf[..., half:]
    return jnp.concatenate([x1 * cos - x2 * sin, x2 * cos + x1 * sin], axis=-1).astype(x.dtype)


def causal_conv(xbc, conv_buf, w, b):
    L = xbc.shape[1]
    xp = jnp.concatenate([conv_buf.astype(xbc.dtype), xbc], axis=1)
    y = b
    for k in range(SSD_CONV):
        y = y + w[k] * xp[:, k:k + L]
    return jax.nn.silu(y), xp[:, xp.shape[1] - (SSD_CONV - 1):]


def ssd_scan(x, dt, A, Bm, Cm, h0):
    f32 = jnp.float32
    b, L, H, P = x.shape
    G, N = Bm.shape[-2:]
    hg = H // G
    Q = math.gcd(L, SSD_CHUNK)
    nc = L // Q

    def to_chunks(t):
        return jnp.moveaxis(t.astype(f32).reshape((b, nc, Q) + t.shape[2:]), 1, 0)

    causal = jnp.tril(jnp.ones((Q, Q), dtype=bool))

    def step(h, inp):
        xc, dtc, bc, cc = inp
        a = jnp.cumsum(dtc * A, axis=1)
        aT = jnp.swapaxes(a, 1, 2)
        seg = jnp.where(causal[None, None], aT[..., :, None] - aT[..., None, :], -jnp.inf)
        decay = jnp.exp(seg).reshape(b, G, hg, Q, Q)
        cb = jnp.einsum('bign,bjgn->bgij', cc, bc)
        xdt = (xc * dtc[..., None]).reshape(b, Q, G, hg, P)
        y_intra = jnp.einsum('bgkij,bjgkp->bigkp', cb[:, :, None] * decay, xdt)
        hr = h.reshape(b, G, hg, P, N)
        y_state = jnp.einsum('bign,bgkpn->bigkp', cc, hr) * jnp.exp(a).reshape(b, Q, G, hg)[..., None]
        y = (y_intra + y_state).reshape(b, Q, H, P)
        a_last = a[:, -1]
        w_end = jnp.exp(a_last[:, None, :] - a).reshape(b, Q, G, hg)
        h_new = h * jnp.exp(a_last)[:, :, None, None] + jnp.einsum(
            'bjgn,bjgkp->bgkpn', bc, xdt * w_end[..., None]).reshape(b, H, P, N)
        return h_new, y

    h, ys = lax.scan(step, h0.astype(f32), (to_chunks(x), to_chunks(dt), to_chunks(Bm), to_chunks(Cm)))
    return jnp.moveaxis(ys, 0, 1).reshape(b, L, H, P), h


def ssd_branch(u, conv_buf, h0, conv_w, conv_b, dt_bias, a_log, d_skip, norm_w):
    z = u[..., :SSD_D_INNER]
    xbc = u[..., SSD_D_INNER:SSD_D_INNER + SSD_CONV_DIM]
    dt_raw = u[..., SSD_D_INNER + SSD_CONV_DIM:]
    xbc, new_buf = causal_conv(xbc, conv_buf, conv_w, conv_b)
    b, L, _ = xbc.shape
    gn = SSD_GROUPS * SSD_STATE
    xs = xbc[..., :SSD_D_INNER].reshape(b, L, SSD_HEADS, SSD_HEAD_DIM)
    Bm = xbc[..., SSD_D_INNER:SSD_D_INNER + gn].reshape(b, L, SSD_GROUPS, SSD_STATE)
    Cm = xbc[..., SSD_D_INNER + gn:].reshape(b, L, SSD_GROUPS, SSD_STATE)
    dt = jax.nn.softplus(dt_raw.astype(jnp.float32) + dt_bias.astype(jnp.float32))
    A = -jnp.exp(a_log.astype(jnp.float32))
    y, h = ssd_scan(xs, dt, A, Bm, Cm, h0)
    y = y + xs.astype(jnp.float32) * d_skip.astype(jnp.float32)[:, None]
    y = y.reshape(b, L, SSD_D_INNER).astype(u.dtype)
    return gated_group_rmsnorm(y, z, norm_w), new_buf, h.astype(u.dtype)


def mla_project(u, pos, q_norm_w, kv_norm_w, w_uq, w_ukv):
    b, L, _ = u.shape
    o_kv = MLA_Q_LORA
    o_pe = o_kv + MLA_KV_LORA
    o_g = o_pe + MLA_ROPE
    q = (rmsnorm(u[..., :o_kv], q_norm_w) @ w_uq).reshape(b, L, MLA_HEADS, MLA_NOPE + MLA_ROPE)
    q_pe = rope(q[..., MLA_NOPE:], pos)
    ckv = rmsnorm(u[..., o_kv:o_pe], kv_norm_w)
    kpe = rope(u[..., o_pe:o_g][:, :, None, :], pos)[:, :, 0]
    w_ukv = w_ukv.reshape(MLA_KV_LORA, MLA_HEADS, MLA_NOPE + MLA_V)
    q_lat = jnp.einsum('blhd,chd->blhc', q[..., :MLA_NOPE], w_ukv[..., :MLA_NOPE])
    return q_lat, q_pe, ckv, kpe, u[..., o_g:], w_ukv[..., MLA_NOPE:]


def mla_merge(o_lat, w_uv, gate):
    b, L = o_lat.shape[:2]
    o = jnp.einsum('blhc,chd->blhd', o_lat, w_uv).reshape(b, L, MLA_WIDTH)
    return o * jax.nn.silu(gate)


def mla_branch_prompt(u, pos, q_norm_w, kv_norm_w, w_uq, w_ukv):
    q_lat, q_pe, ckv, kpe, gate, w_uv = mla_project(u, pos, q_norm_w, kv_norm_w, w_uq, w_ukv)
    b, L, H, C = q_lat.shape
    nb = L // Q_BLOCK
    kpos = jnp.arange(L)

    def to_blocks(t):
        return jnp.moveaxis(t.reshape((b, nb, Q_BLOCK) + t.shape[2:]), 1, 0)

    def block(args):
        ql, qp, qpos = args
        s = (jnp.einsum('bqhc,bkc->bhqk', ql, ckv, preferred_element_type=jnp.float32)
             + jnp.einsum('bqhr,bkr->bhqk', qp, kpe, preferred_element_type=jnp.float32)) * MLA_SCALE
        s = jnp.where((kpos[None, :] <= qpos[:, None])[None, None], s, -jnp.inf)
        p = jax.nn.softmax(s, axis=-1)
        return jnp.einsum('bhqk,bkc->bqhc', p.astype(ckv.dtype), ckv)

    o_blocks = lax.map(block, (to_blocks(q_lat), to_blocks(q_pe), kpos.reshape(nb, Q_BLOCK)))
    o_lat = jnp.moveaxis(o_blocks, 0, 1).reshape(b, L, H, C)
    return mla_merge(o_lat, w_uv, gate), ckv, kpe


def mla_branch_sample(u, pos, ckv_pool, kpe_pool, layer, page_table, q_norm_w, kv_norm_w, w_uq, w_ukv):
    q_lat, q_pe, ckv, kpe, gate, w_uv = mla_project(u, pos, q_norm_w, kv_norm_w, w_uq, w_ukv)
    f32 = jnp.float32
    L = q_lat.shape[1]

    def scores(kc, kr):
        return (jnp.einsum('bqhc,btc->bqht', q_lat, kc, preferred_element_type=f32)
                + jnp.einsum('bqhr,btr->bqht', q_pe, kr, preferred_element_type=f32)) * MLA_SCALE

    causal = jnp.tril(jnp.ones((L, L), dtype=bool))
    s = jnp.where(causal[None, :, None, :], scores(ckv, kpe), -jnp.inf)
    m = jnp.max(s, axis=-1)
    p = jnp.exp(s - m[..., None])
    l = jnp.sum(p, axis=-1)
    acc = jnp.einsum('bqht,btc->bqhc', p, ckv.astype(f32))

    def page_step(carry, pages):
        m, l, acc = carry
        kc = ckv_pool[layer, pages]
        kr = kpe_pool[layer, pages]
        s = scores(kc, kr)
        m_new = jnp.maximum(m, jnp.max(s, axis=-1))
        corr = jnp.exp(m - m_new)
        p = jnp.exp(s - m_new[..., None])
        acc = acc * corr[..., None] + jnp.einsum('bqht,btc->bqhc', p, kc.astype(f32))
        return (m_new, l * corr + jnp.sum(p, axis=-1), acc), None

    (m, l, acc), _ = lax.scan(page_step, (m, l, acc), page_table.T)
    o_lat = (acc / l[..., None]).astype(q_lat.dtype)
    return mla_merge(o_lat, w_uv, gate), ckv, kpe


def memory_kv(mem, mem_norm_w, w_mem_kv):
    b, M, _ = mem.shape
    kv = rmsnorm(mem, mem_norm_w) @ w_mem_kv
    k = kv[..., :X_WIDTH].reshape(b, M, X_HEADS, X_HEAD_DIM)
    v = kv[..., X_WIDTH:].reshape(b, M, X_HEADS, X_HEAD_DIM)
    return k, v


def memory_cross_attend(xq, xg, k, v):
    b, L, _ = xq.shape
    q = xq.reshape(b, L, X_HEADS, X_HEAD_DIM)
    s = jnp.einsum('blhd,bmhd->bhlm', q, k.astype(q.dtype), preferred_element_type=jnp.float32) * X_SCALE
    p = jax.nn.softmax(s, axis=-1)
    o = jnp.einsum('bhlm,bmhd->blhd', p.astype(q.dtype), v.astype(q.dtype)).reshape(b, L, X_WIDTH)
    return o * jax.nn.silu(xg)


def setup_inputs(seed: int = 0) -> dict:
    key = jax.random.key(seed)
    keys = iter(jax.random.split(key, 40))
    f32 = jnp.float32

    def nrm(shape, scale=1.0):
        return jax.random.normal(next(keys), shape, f32) * scale

    def gain(shape):
        return 1.0 + nrm(shape, 0.02)

    n_pages = PAST_LEN // PAGE_SIZE
    n_used = DEC_BATCH * n_pages
    n_pool = n_used + max(1, n_used // 4)
    page_table = jax.random.permutation(next(keys), n_pool)[:n_used].reshape(DEC_BATCH, n_pages).astype(jnp.int32)
    dt0 = jnp.exp(jax.random.uniform(next(keys), (N_SSD, SSD_HEADS), f32, math.log(1e-3), math.log(1e-1)))
    dt_bias = dt0 + jnp.log(-jnp.expm1(-dt0))
    a_log = jnp.log(jax.random.uniform(next(keys), (N_SSD, SSD_HEADS), f32, 1.0, 16.0))
    return {
        'x_prompt': nrm((BATCH, SEQ, D_MODEL)),
        'x_sample': nrm((DEC_BATCH, DEC_SEQ, D_MODEL)),
        'cache_ckv': nrm((N_MLA, n_pool, PAGE_SIZE, MLA_KV_LORA)),
        'cache_kpe': nrm((N_MLA, n_pool, PAGE_SIZE, MLA_ROPE)),
        'state_ssm': nrm((N_SSD, DEC_BATCH, SSD_HEADS, SSD_HEAD_DIM, SSD_STATE), 0.3),
        'state_conv': nrm((N_SSD, DEC_BATCH, SSD_CONV - 1, SSD_CONV_DIM)),
        'cache_mem_k': nrm((DEPTH, DEC_BATCH, N_MEM, X_HEADS, X_HEAD_DIM)),
        'cache_mem_v': nrm((DEPTH, DEC_BATCH, N_MEM, X_HEADS, X_HEAD_DIM)),
        'page_table': page_table,
        'mem_prompt': nrm((BATCH, N_MEM, D_MODEL)),
        'norm_w': gain((DEPTH, D_MODEL)),
        'final_norm_w': gain((D_MODEL,)),
        'mem_norm_w': gain((DEPTH, D_MODEL)),
        'w_mem_kv': nrm((DEPTH, D_MODEL, 2 * X_WIDTH), D_MODEL ** -0.5),
        'ssd_w_in': nrm((N_SSD, D_MODEL, SSD_MIX_COLS + 2 * X_WIDTH), D_MODEL ** -0.5),
        'ssd_conv_w': nrm((N_SSD, SSD_CONV, SSD_CONV_DIM), SSD_CONV ** -0.5),
        'ssd_conv_b': nrm((N_SSD, SSD_CONV_DIM), 0.02),
        'ssd_dt_bias': dt_bias,
        'ssd_A_log': a_log,
        'ssd_D': gain((N_SSD, SSD_HEADS)),
        'ssd_norm_w': gain((N_SSD, SSD_D_INNER)),
        'ssd_w_out': nrm((N_SSD, SSD_D_INNER + X_WIDTH, D_MODEL), (SSD_D_INNER + X_WIDTH) ** -0.5),
        'mla_w_in': nrm((N_MLA, D_MODEL, MLA_MIX_COLS + 2 * X_WIDTH), D_MODEL ** -0.5),
        'mla_q_norm_w': gain((N_MLA, MLA_Q_LORA)),
        'mla_kv_norm_w': gain((N_MLA, MLA_KV_LORA)),
        'mla_w_uq': nrm((N_MLA, MLA_Q_LORA, MLA_HEADS * (MLA_NOPE + MLA_ROPE)), MLA_Q_LORA ** -0.5),
        'mla_w_ukv': nrm((N_MLA, MLA_KV_LORA, MLA_HEADS * (MLA_NOPE + MLA_V)), MLA_KV_LORA ** -0.5),
        'mla_w_out': nrm((N_MLA, MLA_WIDTH + X_WIDTH, D_MODEL), (MLA_WIDTH + X_WIDTH) ** -0.5),
    }


def reference(x_prompt, x_sample, cache_ckv, cache_kpe, state_ssm, state_conv, cache_mem_k, cache_mem_v,
              page_table, mem_prompt, norm_w, final_norm_w, mem_norm_w, w_mem_kv,
              ssd_w_in, ssd_conv_w, ssd_conv_b, ssd_dt_bias, ssd_A_log, ssd_D, ssd_norm_w, ssd_w_out,
              mla_w_in, mla_q_norm_w, mla_kv_norm_w, mla_w_uq, mla_w_ukv, mla_w_out):
    b_p, l_p = x_prompt.shape[:2]
    l_s = x_sample.shape[1]
    pos_p = jnp.arange(l_p)
    pos_s = PAST_LEN + jnp.arange(l_s)
    hp, hs = x_prompt, x_sample
    ckv_p, kpe_p, ckv_s, kpe_s = [], [], [], []
    ssm_p, ssm_s, conv_p, conv_s = [], [], [], []
    mk_out, mv_out = [], []
    for i in range(DEPTH):
        j = i // N_MIXERS
        mk_p, mv_p = memory_kv(mem_prompt, mem_norm_w[i], w_mem_kv[i])
        mk_out.append(mk_p)
        mv_out.append(mv_p)
        up = rmsnorm(hp, norm_w[i])
        us = rmsnorm(hs, norm_w[i])
        if i % N_MIXERS == 0:
            up = up @ ssd_w_in[j]
            us = us @ ssd_w_in[j]
            n_mix = SSD_MIX_COLS
            params = (ssd_conv_w[j], ssd_conv_b[j], ssd_dt_bias[j], ssd_A_log[j], ssd_D[j], ssd_norm_w[j])
            buf0 = jnp.zeros((b_p, SSD_CONV - 1, SSD_CONV_DIM), up.dtype)
            h0 = jnp.zeros((b_p, SSD_HEADS, SSD_HEAD_DIM, SSD_STATE), jnp.float32)
            mix_p, cbuf, hst = ssd_branch(up[..., :n_mix], buf0, h0, *params)
            conv_p.append(cbuf)
            ssm_p.append(hst)
            mix_s, cbuf, hst = ssd_branch(us[..., :n_mix], state_conv[j], state_ssm[j], *params)
            conv_s.append(cbuf)
            ssm_s.append(hst)
            w_out = ssd_w_out[j]
        else:
            up = up @ mla_w_in[j]
            us = us @ mla_w_in[j]
            n_mix = MLA_MIX_COLS
            params = (mla_q_norm_w[j], mla_kv_norm_w[j], mla_w_uq[j], mla_w_ukv[j])
            mix_p, ckv, kpe = mla_branch_prompt(up[..., :n_mix], pos_p, *params)
            ckv_p.append(ckv)
            kpe_p.append(kpe)
            mix_s, ckv, kpe = mla_branch_sample(us[..., :n_mix], pos_s, cache_ckv, cache_kpe, j, page_table, *params)
            ckv_s.append(ckv)
            kpe_s.append(kpe)
            w_out = mla_w_out[j]
        mem_p = memory_cross_attend(up[..., n_mix:n_mix + X_WIDTH], up[..., n_mix + X_WIDTH:], mk_p, mv_p)
        mem_s = memory_cross_attend(us[..., n_mix:n_mix + X_WIDTH], us[..., n_mix + X_WIDTH:],
                                    cache_mem_k[i], cache_mem_v[i])
        hp = hp + jnp.concatenate([mix_p, mem_p], axis=-1) @ w_out
        hs = hs + jnp.concatenate([mix_s, mem_s], axis=-1) @ w_out
    y_prompt = rmsnorm(hp, final_norm_w)
    y_sample = rmsnorm(hs, final_norm_w)
    return (y_prompt, y_sample, jnp.stack(ckv_p), jnp.stack(kpe_p), jnp.stack(ckv_s), jnp.stack(kpe_s),
            jnp.stack(ssm_p), jnp.stack(ssm_s), jnp.stack(conv_p), jnp.stack(conv_s),
            jnp.stack(mk_out), jnp.stack(mv_out))
```

```python
import functools
import math

import jax
import jax.numpy as jnp
from jax import lax
from jax.experimental import pallas as pl
from jax.experimental.pallas import tpu as pltpu

F32 = jnp.float32
BF16 = jnp.bfloat16

D_MODEL = 4096
BATCH = 4
SEQ = 2048
DEPTH = 4
DEC_BATCH = 128
PAST_LEN = 8192
PAGE_SIZE = 128
N_PAGES = PAST_LEN // PAGE_SIZE
EPS = 1e-6

N_MEM = 256
X_HEADS = 4
X_HEAD_DIM = 768
X_WIDTH = X_HEADS * X_HEAD_DIM
X_SCALE = X_HEAD_DIM ** -0.5

SSD_D_INNER = 2 * D_MODEL
SSD_HEAD_DIM = 64
SSD_HEADS = SSD_D_INNER // SSD_HEAD_DIM
SSD_GROUPS = 8
SSD_HEADS_PER_GROUP = SSD_HEADS // SSD_GROUPS
SSD_GROUP_WIDTH = SSD_D_INNER // SSD_GROUPS
SSD_STATE = 128
SSD_CONV = 4
SSD_CONV_DIM = SSD_D_INNER + 2 * SSD_GROUPS * SSD_STATE
SSD_CHUNK = 128
SSD_MIX_COLS = SSD_D_INNER + SSD_CONV_DIM + SSD_HEADS

MLA_HEADS = 32
MLA_Q_LORA = 1024
MLA_KV_LORA = 512
MLA_NOPE = 128
MLA_ROPE = 64
MLA_V = 128
MLA_WIDTH = MLA_HEADS * MLA_V
MLA_MIX_COLS = MLA_Q_LORA + MLA_KV_LORA + MLA_ROPE + MLA_WIDTH
MLA_SCALE = (MLA_NOPE + MLA_ROPE) ** -0.5
ROPE_THETA = 10000.0

N_PROMPT = BATCH * SEQ
N_TOK = N_PROMPT + DEC_BATCH

V7X_LANES = 128
V7X_VMEM_LIMIT_BYTES = 56 * 1024 * 1024
V7X_VMEM_TILE_BUDGET_BYTES = 44 * 1024 * 1024


def _cparams(semantics):
    return pltpu.CompilerParams(dimension_semantics=semantics, vmem_limit_bytes=V7X_VMEM_LIMIT_BYTES)


def _sigmoid(x):
    return 1.0 / (1.0 + jnp.exp(-x))


def _silu(x):
    return x * _sigmoid(x)


def _dot(a, b):
    return jnp.dot(a, b, preferred_element_type=F32)


def _dot_nt(a, b):
    return lax.dot_general(a, b, (((1,), (1,)), ((), ())), preferred_element_type=F32)


def _split3(x):
    hi = x.astype(BF16)
    r1 = x - hi.astype(F32)
    mid = r1.astype(BF16)
    lo = (r1 - mid.astype(F32)).astype(BF16)
    return hi, mid, lo


def _dot3(x, e):
    hi, mid, lo = _split3(x)
    return _dot(hi, e) + _dot(mid, e) + _dot(lo, e)


def _dot3_left(e, x):
    hi, mid, lo = _split3(x)
    return _dot(e, hi) + _dot(e, mid) + _dot(e, lo)


def _row_block(m, target):
    best = None
    for d in range(16, min(m, target) + 1, 16):
        if m % d == 0:
            best = d
    assert best is not None, (m, target)
    return best


def _rmsnorm_body(x_ref, w_ref, *o_refs):
    x = x_ref[...]
    y = x * lax.rsqrt(jnp.mean(x * x, axis=-1, keepdims=True) + EPS) * w_ref[...]
    for o_ref in o_refs:
        o_ref[...] = y.astype(o_ref.dtype)


def rmsnorm(x, w, out_dtypes, width=None, col_block=0):
    m = x.shape[0]
    d = width or x.shape[1]
    bm = _row_block(m, max(16, (2 * 1024 * 1024) // (d * 4)))
    outs = pl.pallas_call(
        _rmsnorm_body,
        grid=(m // bm,),
        in_specs=[pl.BlockSpec((bm, d), lambda i: (i, col_block)),
                  pl.BlockSpec((1, d), lambda i: (0, 0))],
        out_specs=[pl.BlockSpec((bm, d), lambda i: (i, 0)) for _ in out_dtypes],
        out_shape=[jax.ShapeDtypeStruct((m, d), dt) for dt in out_dtypes],
        compiler_params=_cparams(("parallel",)),
        name="rmsnorm",
    )(x, w.reshape(1, d))
    return outs


def _mm_body(*refs, nk, has_res):
    if has_res:
        x_ref, w_ref, r_ref, o_ref, *scratch = refs
    else:
        x_ref, w_ref, o_ref, *scratch = refs
        r_ref = None

    def finish(acc):
        if has_res:
            acc = acc + r_ref[...]
        o_ref[...] = acc.astype(o_ref.dtype)

    if nk == 1:
        finish(_dot(x_ref[...], w_ref[...]))
    else:
        acc_ref, = scratch
        k = pl.program_id(2)

        @pl.when(k == 0)
        def _():
            acc_ref[...] = jnp.zeros_like(acc_ref)

        acc_ref[...] += _dot(x_ref[...], w_ref[...])

        @pl.when(k == nk - 1)
        def _():
            finish(acc_ref[...])


def _mm_blocks(m, k, n, out_bytes, has_res):
    bk = k
    if k > 4096:
        for cand in (2048, 1024, 512):
            if k % cand == 0:
                bk = cand
                break
    bn_cands = [c for c in (1024, 768, 512, 384, 256, 128) if n % c == 0]
    if n <= 2048 and n not in bn_cands:
        bn_cands = [n] + bn_cands
    best = None
    for bn in bn_cands:
        for bm_target in (1040, 640, 520, 320, 208, 128):
            bm = _row_block(m, bm_target)
            need = 2 * (bm * bk * 2 + bk * bn * 2 + bm * bn * out_bytes)
            if bk != k:
                need += bm * bn * 4
            if has_res:
                need += 2 * bm * bn * 4
            if need <= V7X_VMEM_TILE_BUDGET_BYTES:
                cand = (bm * bn, bm, bn)
                if best is None or cand > best:
                    best = cand
                break
    assert best is not None, (m, k, n)
    return best[1], best[2], bk


def matmul(x, w, out_dtype, res=None):
    m, k = x.shape
    n = w.shape[1]
    has_res = res is not None
    bm, bn, bk = _mm_blocks(m, k, n, jnp.dtype(out_dtype).itemsize, has_res)
    nk = k // bk
    in_specs = [pl.BlockSpec((bm, bk), lambda j, i, kk: (i, kk)),
                pl.BlockSpec((bk, bn), lambda j, i, kk: (kk, j))]
    args = [x, w]
    if has_res:
        in_specs.append(pl.BlockSpec((bm, bn), lambda j, i, kk: (i, j)))
        args.append(res)
    return pl.pallas_call(
        functools.partial(_mm_body, nk=nk, has_res=has_res),
        grid=(n // bn, m // bm, nk),
        in_specs=in_specs,
        out_specs=pl.BlockSpec((bm, bn), lambda j, i, kk: (i, j)),
        out_shape=jax.ShapeDtypeStruct((m, n), out_dtype),
        scratch_shapes=[pltpu.VMEM((bm, bn), F32)] if nk > 1 else [],
        compiler_params=_cparams(("parallel", "parallel", "arbitrary")),
        name="matmul",
    )(*args)


def _rope_tables():
    half = MLA_ROPE // 2
    inv = ROPE_THETA ** (-jnp.arange(half, dtype=F32) / half)
    pos = jnp.concatenate([jnp.tile(jnp.arange(SEQ), BATCH),
                           jnp.full((DEC_BATCH,), PAST_LEN, jnp.int32)]).astype(F32)
    ang = pos[:, None] * inv[None, :]
    cos = jnp.cos(ang)
    sin = jnp.sin(ang)
    cos2 = jnp.tile(cos, (1, V7X_LANES // half))
    sin2 = jnp.tile(jnp.concatenate([-sin, sin], axis=1), (1, V7X_LANES // MLA_ROPE))
    return cos2, sin2


def _rotate_pairs(x, cos2, sin2, first_half):
    swapped = jnp.where(first_half, pltpu.roll(x, V7X_LANES - MLA_ROPE // 2, 1), pltpu.roll(x, MLA_ROPE // 2, 1))
    return x * cos2 + swapped * sin2


def _rope_body(x_ref, c_ref, s_ref, o_ref):
    cos2 = c_ref[...]
    sin2 = s_ref[...]
    lane = lax.broadcasted_iota(jnp.int32, cos2.shape, 1)
    first_half = (lane % MLA_ROPE) < (MLA_ROPE // 2)
    for t in range(x_ref.shape[1] // V7X_LANES):
        sl = slice(t * V7X_LANES, (t + 1) * V7X_LANES)
        o_ref[:, sl] = _rotate_pairs(x_ref[:, sl], cos2, sin2, first_half).astype(o_ref.dtype)


def rope_heads(x, cos2, sin2, out_dtype):
    m, n = x.shape
    bm = _row_block(m, 320)
    return pl.pallas_call(
        _rope_body,
        grid=(m // bm,),
        in_specs=[pl.BlockSpec((bm, n), lambda i: (i, 0)),
                  pl.BlockSpec((bm, V7X_LANES), lambda i: (i, 0)),
                  pl.BlockSpec((bm, V7X_LANES), lambda i: (i, 0))],
        out_specs=pl.BlockSpec((bm, n), lambda i: (i, 0)),
        out_shape=jax.ShapeDtypeStruct((m, n), out_dtype),
        compiler_params=_cparams(("parallel",)),
        name="rope_heads",
    )(x, cos2, sin2)


MLA_A_COLS = MLA_Q_LORA + MLA_KV_LORA + V7X_LANES


def _mla_prep_body(ckv_in_ref, kpe_in_ref, w_ref, c_ref, s_ref, ckv_ref, ckvb_ref, kpe_ref, kpeb_ref):
    x = ckv_in_ref[...]
    y = x * lax.rsqrt(jnp.mean(x * x, axis=-1, keepdims=True) + EPS) * w_ref[...]
    ckv_ref[...] = y
    ckvb_ref[...] = y.astype(BF16)
    cos2 = c_ref[...]
    lane = lax.broadcasted_iota(jnp.int32, cos2.shape, 1)
    first_half = (lane % MLA_ROPE) < (MLA_ROPE // 2)
    r = _rotate_pairs(kpe_in_ref[...], cos2, s_ref[...], first_half)
    kpe_ref[...] = r[:, :MLA_ROPE]
    kpeb_ref[...] = r.astype(BF16)


def mla_prep(a, kv_norm_w, cos2, sin2):
    m = a.shape[0]
    bm = _row_block(m, 640)
    row = lambda i: (i, 0)
    return pl.pallas_call(
        _mla_prep_body,
        grid=(m // bm,),
        in_specs=[pl.BlockSpec((bm, MLA_KV_LORA), lambda i: (i, MLA_Q_LORA // MLA_KV_LORA)),
                  pl.BlockSpec((bm, V7X_LANES), lambda i: (i, (MLA_Q_LORA + MLA_KV_LORA) // V7X_LANES)),
                  pl.BlockSpec((1, MLA_KV_LORA), lambda i: (0, 0)),
                  pl.BlockSpec((bm, V7X_LANES), row),
                  pl.BlockSpec((bm, V7X_LANES), row)],
        out_specs=[pl.BlockSpec((bm, MLA_KV_LORA), row), pl.BlockSpec((bm, MLA_KV_LORA), row),
                   pl.BlockSpec((bm, MLA_ROPE), row), pl.BlockSpec((bm, V7X_LANES), row)],
        out_shape=[jax.ShapeDtypeStruct((m, MLA_KV_LORA), F32), jax.ShapeDtypeStruct((m, MLA_KV_LORA), BF16),
                   jax.ShapeDtypeStruct((m, MLA_ROPE), F32), jax.ShapeDtypeStruct((m, V7X_LANES), BF16)],
        compiler_params=_cparams(("parallel",)),
        name="mla_prep",
    )(a, a, kv_norm_w.reshape(1, MLA_KV_LORA), cos2, sin2)


FLASH_TQ = 256
FLASH_TK = 256


def _flash_body(qn_ref, qp_ref, kv_ref, kpe_ref, g_ref, o_ref):
    qi = pl.program_id(2)
    tq, tk = FLASH_TQ, FLASH_TK
    lane = lax.broadcasted_iota(jnp.int32, (tq, V7X_LANES), 1)
    qp = qp_ref[...]
    row = lax.broadcasted_iota(jnp.int32, (tq, tk), 0)
    col = lax.broadcasted_iota(jnp.int32, (tq, tk), 1)
    for h in range(2):
        own = (lane < MLA_ROPE) if h == 0 else (lane >= MLA_ROPE)
        q = jnp.concatenate([qn_ref[:, h * MLA_NOPE:(h + 1) * MLA_NOPE],
                             jnp.where(own, qp, jnp.zeros_like(qp))], axis=1)
        kbase = h * (MLA_NOPE + MLA_V)

        def step(j, carry, masked):
            m_prev, l_prev, acc = carry
            off = pl.multiple_of(j * tk, tk)
            k = jnp.concatenate([kv_ref[pl.ds(off, tk), kbase:kbase + MLA_NOPE],
                                 kpe_ref[pl.ds(off, tk), :]], axis=1)
            v = kv_ref[pl.ds(off, tk), kbase + MLA_NOPE:kbase + MLA_NOPE + MLA_V]
            s = _dot_nt(q, k) * MLA_SCALE
            if masked:
                s = jnp.where(col <= row, s, -jnp.inf)
            m_new = jnp.maximum(m_prev, jnp.max(s, axis=1, keepdims=True))
            alpha = jnp.exp(m_prev - m_new)
            p = jnp.exp(s - m_new)
            l_new = l_prev * alpha + jnp.sum(p, axis=1, keepdims=True)
            acc = acc * alpha + _dot(p.astype(BF16), v)
            return m_new, l_new, acc

        init = (jnp.full((tq, 1), -jnp.inf, F32), jnp.zeros((tq, 1), F32), jnp.zeros((tq, MLA_V), F32))
        carry = lax.fori_loop(0, qi, functools.partial(step, masked=False), init)
        _, l_fin, acc = step(qi, carry, True)
        sl = slice(h * MLA_V, (h + 1) * MLA_V)
        o_ref[:, sl] = (acc / l_fin * _silu(g_ref[:, sl])).astype(o_ref.dtype)


def mla_flash_prompt(qn, qp, kvup, kpe_dup, gate):
    nq = SEQ // FLASH_TQ
    row_blk = lambda b, hp, qi: (b * nq + qi, hp)
    return pl.pallas_call(
        _flash_body,
        grid=(BATCH, MLA_HEADS // 2, nq),
        in_specs=[pl.BlockSpec((FLASH_TQ, 2 * MLA_NOPE), row_blk),
                  pl.BlockSpec((FLASH_TQ, V7X_LANES), row_blk),
                  pl.BlockSpec((SEQ, 2 * (MLA_NOPE + MLA_V)), lambda b, hp, qi: (b, hp)),
                  pl.BlockSpec((SEQ, V7X_LANES), lambda b, hp, qi: (b, 0)),
                  pl.BlockSpec((FLASH_TQ, 2 * MLA_V), row_blk)],
        out_specs=pl.BlockSpec((FLASH_TQ, 2 * MLA_V), row_blk),
        out_shape=jax.ShapeDtypeStruct((N_PROMPT, MLA_WIDTH), BF16),
        compiler_params=_cparams(("parallel", "parallel", "arbitrary")),
        name="mla_flash_prompt",
    )(qn, qp, kvup, kpe_dup, gate)


PAGES_PER_STEP = 8


def _paged_body(pt_ref, ql_ref, qp_ref, cn_ref, kn_ref, *rest):
    del pt_ref
    pps = PAGES_PER_STEP
    ck_refs = rest[:pps]
    kp_refs = rest[pps:2 * pps]
    o_ref = rest[2 * pps]
    m_ref, l_ref, acc_ref = rest[2 * pps + 1:]
    step = pl.program_id(1)
    ql = ql_ref[...]
    qp = qp_ref[...]

    @pl.when(step == 0)
    def _():
        cn = cn_ref[...]
        s0 = (jnp.sum(ql.astype(F32) * cn, axis=1, keepdims=True)
              + jnp.sum(qp.astype(F32) * kn_ref[...], axis=1, keepdims=True)) * MLA_SCALE
        m_ref[...] = s0
        l_ref[...] = jnp.ones_like(l_ref)
        acc_ref[...] = jnp.broadcast_to(cn, acc_ref.shape)

    kcs = [r[...].astype(BF16) for r in ck_refs]
    s = jnp.concatenate([_dot_nt(ql, kc) + _dot_nt(qp, kr[...].astype(BF16))
                         for kc, kr in zip(kcs, kp_refs)], axis=1) * MLA_SCALE
    m_prev = m_ref[...]
    m_new = jnp.maximum(m_prev, jnp.max(s, axis=1, keepdims=True))
    alpha = jnp.exp(m_prev - m_new)
    p = jnp.exp(s - m_new)
    l_ref[...] = l_ref[...] * alpha + jnp.sum(p, axis=1, keepdims=True)
    pb = p.astype(BF16)
    acc = acc_ref[...] * alpha
    for i, kc in enumerate(kcs):
        acc = acc + _dot(pb[:, i * PAGE_SIZE:(i + 1) * PAGE_SIZE], kc)
    acc_ref[...] = acc
    m_ref[...] = m_new

    @pl.when(step == pl.num_programs(1) - 1)
    def _():
        o_ref[...] = acc_ref[...] / l_ref[...]


def mla_paged_sample(q_lat, q_pe, ckv_new, kpe_new, cache_ckv, cache_kpe, page_table, layer):
    pps = PAGES_PER_STEP
    nsteps = N_PAGES // pps
    per_b = lambda b, s, pt: (b, 0, 0)

    def page_map(i):
        return lambda b, s, pt: (layer, pt[b * N_PAGES + s * pps + i], 0, 0)

    in_specs = [pl.BlockSpec((None, MLA_HEADS, MLA_KV_LORA), per_b),
                pl.BlockSpec((None, MLA_HEADS, MLA_ROPE), per_b),
                pl.BlockSpec((None, 1, MLA_KV_LORA), per_b),
                pl.BlockSpec((None, 1, MLA_ROPE), per_b)]
    in_specs += [pl.BlockSpec((None, None, PAGE_SIZE, MLA_KV_LORA), page_map(i)) for i in range(pps)]
    in_specs += [pl.BlockSpec((None, None, PAGE_SIZE, MLA_ROPE), page_map(i)) for i in range(pps)]
    grid_spec = pltpu.PrefetchScalarGridSpec(
        num_scalar_prefetch=1,
        grid=(DEC_BATCH, nsteps),
        in_specs=in_specs,
        out_specs=pl.BlockSpec((None, MLA_HEADS, MLA_KV_LORA), per_b),
        scratch_shapes=[pltpu.VMEM((MLA_HEADS, 1), F32), pltpu.VMEM((MLA_HEADS, 1), F32),
                        pltpu.VMEM((MLA_HEADS, MLA_KV_LORA), F32)],
    )
    return pl.pallas_call(
        _paged_body,
        grid_spec=grid_spec,
        out_shape=jax.ShapeDtypeStruct((DEC_BATCH, MLA_HEADS, MLA_KV_LORA), F32),
        compiler_params=_cparams(("parallel", "arbitrary")),
        name="mla_paged_sample",
    )(page_table.reshape(-1), q_lat, q_pe, ckv_new, kpe_new,
      *([cache_ckv] * pps), *([cache_kpe] * pps))


def _absorb_body(x_ref, w_ref, o_ref):
    o_ref[...] = _dot_nt(x_ref[...], w_ref[...]).astype(o_ref.dtype)


def mla_absorb_q(q_nope, w_ukv):
    m = q_nope.shape[0]
    return pl.pallas_call(
        _absorb_body,
        grid=(MLA_HEADS,),
        in_specs=[pl.BlockSpec((m, MLA_NOPE), lambda h: (0, h)),
                  pl.BlockSpec((MLA_KV_LORA, MLA_NOPE), lambda h: (0, 2 * h))],
        out_specs=pl.BlockSpec((m, MLA_KV_LORA), lambda h: (0, h)),
        out_shape=jax.ShapeDtypeStruct((m, MLA_HEADS * MLA_KV_LORA), BF16),
        compiler_params=_cparams(("parallel",)),
        name="mla_absorb_q",
    )(q_nope, w_ukv)


def _merge_body(x_ref, w_ref, g_ref, o_ref):
    o = _dot(x_ref[...].astype(BF16), w_ref[...])
    o_ref[...] = (o * _silu(g_ref[...])).astype(o_ref.dtype)


def mla_merge_sample(o_lat, w_ukv, gate):
    m = o_lat.shape[0]
    return pl.pallas_call(
        _merge_body,
        grid=(MLA_HEADS,),
        in_specs=[pl.BlockSpec((m, MLA_KV_LORA), lambda h: (0, h)),
                  pl.BlockSpec((MLA_KV_LORA, MLA_V), lambda h: (0, 2 * h + 1)),
                  pl.BlockSpec((m, MLA_V), lambda h: (0, h))],
        out_specs=pl.BlockSpec((m, MLA_V), lambda h: (0, h)),
        out_shape=jax.ShapeDtypeStruct((m, MLA_WIDTH), BF16),
        compiler_params=_cparams(("parallel",)),
        name="mla_merge_sample",
    )(o_lat, w_ukv, gate)


CROSS_TQ = 256


def _cross_prompt_body(q_ref, g_ref, k_ref, v_ref, o_ref):
    for h in range(X_HEADS):
        sl = slice(h * X_HEAD_DIM, (h + 1) * X_HEAD_DIM)
        s = _dot_nt(q_ref[:, sl], k_ref[:, sl].astype(BF16)) * X_SCALE
        m = jnp.max(s, axis=1, keepdims=True)
        p = jnp.exp(s - m)
        p = p / jnp.sum(p, axis=1, keepdims=True)
        o = _dot(p.astype(BF16), v_ref[:, sl].astype(BF16))
        o_ref[:, sl] = (o * _silu(g_ref[:, sl])).astype(o_ref.dtype)


def cross_attend_prompt(xq, xg, mk, mv):
    nq = SEQ // CROSS_TQ
    row_blk = lambda b, qi: (b * nq + qi, 0)
    return pl.pallas_call(
        _cross_prompt_body,
        grid=(BATCH, nq),
        in_specs=[pl.BlockSpec((CROSS_TQ, X_WIDTH), row_blk),
                  pl.BlockSpec((CROSS_TQ, X_WIDTH), row_blk),
                  pl.BlockSpec((N_MEM, X_WIDTH), lambda b, qi: (b, 0)),
                  pl.BlockSpec((N_MEM, X_WIDTH), lambda b, qi: (b, 0))],
        out_specs=pl.BlockSpec((CROSS_TQ, X_WIDTH), row_blk),
        out_shape=jax.ShapeDtypeStruct((N_PROMPT, X_WIDTH), BF16),
        compiler_params=_cparams(("parallel", "parallel")),
        name="cross_attend_prompt",
    )(xq, xg, mk, mv)


CROSS_ROWS = 16


def _cross_sample_body(q_ref, g_ref, k_ref, v_ref, o_ref):
    rows = lax.broadcasted_iota(jnp.int32, (CROSS_ROWS, X_WIDTH), 0)
    lanes = lax.broadcasted_iota(jnp.int32, (CROSS_ROWS, X_WIDTH), 1)
    own = (lanes >= rows * X_HEAD_DIM) & (lanes < (rows + 1) * X_HEAD_DIM)
    q = jnp.broadcast_to(q_ref[...].astype(F32), (CROSS_ROWS, X_WIDTH))
    qbd = jnp.where(own, q, 0.0).astype(BF16)
    s = _dot_nt(qbd, k_ref[...].astype(BF16)) * X_SCALE
    m = jnp.max(s, axis=1, keepdims=True)
    p = jnp.exp(s - m)
    p = p / jnp.sum(p, axis=1, keepdims=True)
    o_full = _dot(p.astype(BF16), v_ref[...].astype(BF16))
    o = jnp.sum(jnp.where(own, o_full, 0.0), axis=0, keepdims=True)
    o_ref[...] = (o * _silu(g_ref[...])).astype(o_ref.dtype)


def cross_attend_sample(xq_s, xg_s, cache_k, cache_v, layer):
    per_b = lambda b: (b, 0, 0)
    cache_blk = lambda b: (layer, b, 0, 0)
    return pl.pallas_call(
        _cross_sample_body,
        grid=(DEC_BATCH,),
        in_specs=[pl.BlockSpec((None, 1, X_WIDTH), per_b),
                  pl.BlockSpec((None, 1, X_WIDTH), per_b),
                  pl.BlockSpec((None, None, N_MEM, X_WIDTH), cache_blk),
                  pl.BlockSpec((None, None, N_MEM, X_WIDTH), cache_blk)],
        out_specs=pl.BlockSpec((None, 1, X_WIDTH), per_b),
        out_shape=jax.ShapeDtypeStruct((DEC_BATCH, 1, X_WIDTH), BF16),
        compiler_params=_cparams(("parallel",)),
        name="cross_attend_sample",
    )(xq_s, xg_s, cache_k, cache_v)


SSD_CARRY_ROWS = 8


def _softplus(x):
    return jnp.maximum(x, 0.0) + jnp.log(1.0 + jnp.exp(-jnp.abs(x)))


def _ssd_prompt_body(z_ref, xbc_ref, dt_ref, cw_ref, cb_ref, dtb_ref, alog_ref, dexp_ref, nw_ref,
                     o_ref, ssm_ref, xs_ref, ht_ref, at_ref):
    c = pl.program_id(1)
    q = SSD_CHUNK
    gw = SSD_GROUP_WIDTH
    hpg = SSD_HEADS_PER_GROUP
    base = SSD_CARRY_ROWS

    @pl.when(c == 0)
    def _():
        ht_ref[...] = jnp.zeros_like(ht_ref)
        xs_ref[0:base, :] = jnp.zeros((base, SSD_CONV_DIM), F32)

    xs_ref[base:base + q, :] = xbc_ref[...]

    def conv(off, width):
        cols = pl.ds(off, width)
        acc = cb_ref[:, cols]
        for tap in range(SSD_CONV):
            lo = base - (SSD_CONV - 1) + tap
            acc = acc + cw_ref[tap:tap + 1, cols] * xs_ref[lo:lo + q, cols]
        return _silu(acc)

    dt = _softplus(dt_ref[...] + dtb_ref[...])
    da = dt * (-jnp.exp(alog_ref[...]))
    ti = lax.broadcasted_iota(jnp.int32, (q, q), 0)
    tj = lax.broadcasted_iota(jnp.int32, (q, q), 1)
    causal = tj <= ti
    tri = jnp.where(causal, 1.0, 0.0).astype(BF16)
    a = _dot3_left(tri, da)
    at_ref[...] = a.T

    head_row = lax.broadcasted_iota(jnp.int32, (SSD_HEADS, gw), 0)
    exp_lane = lax.broadcasted_iota(jnp.int32, (SSD_HEADS, gw), 1) // SSD_HEAD_DIM
    col_row = lax.broadcasted_iota(jnp.int32, (SSD_HEADS, hpg * q), 0)
    col_lane = lax.broadcasted_iota(jnp.int32, (SSD_HEADS, hpg * q), 1) // q
    pair_lane = lax.broadcasted_iota(jnp.int32, (q, 2 * SSD_HEAD_DIM), 1)

    def group(g, _):
        ox = pl.multiple_of(g * gw, gw)
        ob = pl.multiple_of(SSD_D_INNER + g * SSD_STATE, SSD_STATE)
        oc = pl.multiple_of(SSD_D_INNER + SSD_GROUPS * SSD_STATE + g * SSD_STATE, SSD_STATE)
        xg = conv(ox, gw)
        bg = conv(ob, SSD_STATE)
        cg = conv(oc, SSD_STATE)
        bgb = bg.astype(BF16)
        cgb = cg.astype(BF16)

        e_hp = jnp.where(head_row == g * hpg + exp_lane, 1.0, 0.0).astype(BF16)
        dt_e = _dot3(dt, e_hp)
        a_e = _dot3(a, e_hp)
        a_last = a_e[q - 1:q, :]
        xdt = xg * dt_e
        xw = xdt * jnp.exp(a_last - a_e)
        xdtb = xdt.astype(BF16)

        cb = _dot_nt(cgb, bgb)
        e_col = jnp.where(col_row == g * hpg + col_lane, 1.0, 0.0).astype(BF16)
        a_col = _dot3(a, e_col)

        pieces = []
        for kp in range(hpg // 2):
            xpair = xdtb[:, kp * 2 * SSD_HEAD_DIM:(kp + 1) * 2 * SSD_HEAD_DIM]
            ypair = jnp.zeros((q, 2 * SSD_HEAD_DIM), F32)
            for hh in range(2):
                k = 2 * kp + hh
                a_row = at_ref[pl.ds(g * hpg + k, 1), :]
                seg = a_col[:, k * q:(k + 1) * q] - a_row
                decay = jnp.exp(jnp.where(causal, seg, -jnp.inf))
                own = (pair_lane < SSD_HEAD_DIM) if hh == 0 else (pair_lane >= SSD_HEAD_DIM)
                ypair = ypair + _dot((cb * decay).astype(BF16), jnp.where(own, xpair, jnp.zeros_like(xpair)))
            pieces.append(ypair)
        y_intra = jnp.concatenate(pieces, axis=1)

        ht_g = ht_ref[:, pl.ds(ox, gw)]
        y = y_intra + _dot(cgb, ht_g.astype(BF16)) * jnp.exp(a_e) + xg * dexp_ref[:, pl.ds(ox, gw)]
        zg = z_ref[:, pl.ds(ox, gw)]
        gt = y * _silu(zg)
        gt = gt * lax.rsqrt(jnp.mean(gt * gt, axis=1, keepdims=True) + EPS) * nw_ref[:, pl.ds(ox, gw)]
        o_ref[:, pl.ds(ox, gw)] = gt.astype(o_ref.dtype)

        ht_ref[:, pl.ds(ox, gw)] = ht_g * jnp.exp(a_last) + _dot(bg.T.astype(BF16), xw.astype(BF16))
        return 0

    lax.fori_loop(0, SSD_GROUPS, group, 0)
    xs_ref[0:base, :] = xs_ref[q:q + base, :]

    @pl.when(c == pl.num_programs(1) - 1)
    def _():
        for g in range(SSD_GROUPS):
            ssm_ref[g * gw:(g + 1) * gw, :] = ht_ref[:, g * gw:(g + 1) * gw].T


def ssd_prompt(z, xbc, dt_raw, conv_w, conv_b, dt_bias, a_log, d_skip, norm_w):
    nc = SEQ // SSD_CHUNK
    row_blk = lambda b, c: (b * nc + c, 0)
    const = lambda b, c: (0, 0)
    d_exp = jnp.repeat(d_skip, SSD_HEAD_DIM).reshape(1, SSD_D_INNER)
    return pl.pallas_call(
        _ssd_prompt_body,
        grid=(BATCH, nc),
        in_specs=[pl.BlockSpec((SSD_CHUNK, SSD_D_INNER), row_blk),
                  pl.BlockSpec((SSD_CHUNK, SSD_CONV_DIM), row_blk),
                  pl.BlockSpec((SSD_CHUNK, SSD_HEADS), row_blk),
                  pl.BlockSpec((SSD_CONV, SSD_CONV_DIM), const),
                  pl.BlockSpec((1, SSD_CONV_DIM), const),
                  pl.BlockSpec((1, SSD_HEADS), const),
                  pl.BlockSpec((1, SSD_HEADS), const),
                  pl.BlockSpec((1, SSD_D_INNER), const),
                  pl.BlockSpec((1, SSD_D_INNER), const)],
        out_specs=[pl.BlockSpec((SSD_CHUNK, SSD_D_INNER), row_blk),
                   pl.BlockSpec((None, SSD_D_INNER, SSD_STATE), lambda b, c: (b, 0, 0))],
        out_shape=[jax.ShapeDtypeStruct((N_PROMPT, SSD_D_INNER), BF16),
                   jax.ShapeDtypeStruct((BATCH, SSD_D_INNER, SSD_STATE), F32)],
        scratch_shapes=[pltpu.VMEM((SSD_CARRY_ROWS + SSD_CHUNK, SSD_CONV_DIM), F32),
                        pltpu.VMEM((SSD_STATE, SSD_D_INNER), F32),
                        pltpu.VMEM((SSD_HEADS, SSD_CHUNK), F32)],
        compiler_params=_cparams(("parallel", "arbitrary")),
        name="ssd_prompt",
    )(z, xbc, dt_raw, conv_w, conv_b.reshape(1, -1), dt_bias.reshape(1, -1), a_log.reshape(1, -1),
      d_exp, norm_w.reshape(1, -1))


SSD_PREP_COLS = 1024


def _ssd_conv_step_body(x_ref, st_ref, cw_ref, cb_ref, o_ref, st_out_ref):
    x = x_ref[...]
    acc = cb_ref[...] + cw_ref[SSD_CONV - 1:SSD_CONV, :] * x
    for tap in range(SSD_CONV - 1):
        acc = acc + cw_ref[tap:tap + 1, :] * st_ref[tap]
    o_ref[...] = _silu(acc)
    for tap in range(SSD_CONV - 2):
        st_out_ref[tap] = st_ref[tap + 1]
    st_out_ref[SSD_CONV - 2] = x


def ssd_conv_step(xbc_s, conv_state_t, conv_w, conv_b):
    w = SSD_PREP_COLS
    return pl.pallas_call(
        _ssd_conv_step_body,
        grid=(SSD_CONV_DIM // w,),
        in_specs=[pl.BlockSpec((DEC_BATCH, w), lambda j: (0, j)),
                  pl.BlockSpec((SSD_CONV - 1, DEC_BATCH, w), lambda j: (0, 0, j)),
                  pl.BlockSpec((SSD_CONV, w), lambda j: (0, j)),
                  pl.BlockSpec((1, w), lambda j: (0, j))],
        out_specs=[pl.BlockSpec((DEC_BATCH, w), lambda j: (0, j)),
                   pl.BlockSpec((SSD_CONV - 1, DEC_BATCH, w), lambda j: (0, 0, j))],
        out_shape=[jax.ShapeDtypeStruct((DEC_BATCH, SSD_CONV_DIM), F32),
                   jax.ShapeDtypeStruct((SSD_CONV - 1, DEC_BATCH, SSD_CONV_DIM), F32)],
        compiler_params=_cparams(("parallel",)),
        name="ssd_conv_step",
    )(xbc_s, conv_state_t, conv_w, conv_b.reshape(1, -1))


SSD_STEP_LANES = 1024


def _ssd_state_step_body(s_ref, x_ref, b_ref, c_ref, dtr_ref, dtb_ref, alog_ref, dcol_ref, s_out_ref, y_ref):
    hd, n = SSD_HEAD_DIM, SSD_STATE
    dt_row = _softplus(dtr_ref[...] + dtb_ref[...])
    dec_row = jnp.exp(dt_row * (-jnp.exp(alog_ref[...])))
    dt_col = jnp.broadcast_to(dt_row, (SSD_HEADS, SSD_HEADS)).T
    dec_col = jnp.broadcast_to(dec_row, (SSD_HEADS, SSD_HEADS)).T
    x = x_ref[...]
    xdt = (x * dt_col[:, :hd]).astype(BF16)
    bh = b_ref[...]
    ch = c_ref[...]
    reps = SSD_STEP_LANES // n
    dec_t = jnp.concatenate([dec_col] * reps, axis=1)
    bh_t = jnp.concatenate([bh] * reps, axis=1)
    ch_t = jnp.concatenate([ch] * reps, axis=1)
    p_row = lax.broadcasted_iota(jnp.int32, (hd, SSD_STEP_LANES), 0)
    p_lane = lax.broadcasted_iota(jnp.int32, (hd, SSD_STEP_LANES), 1) // n
    r_row = lax.broadcasted_iota(jnp.int32, (SSD_STEP_LANES, hd), 0) // n
    r_lane = lax.broadcasted_iota(jnp.int32, (SSD_STEP_LANES, hd), 1)
    y = x * dcol_ref[...]
    for t in range(hd * n // SSD_STEP_LANES):
        sl = slice(t * SSD_STEP_LANES, (t + 1) * SSD_STEP_LANES)
        e_p = jnp.where(p_row == t * reps + p_lane, 1.0, 0.0).astype(BF16)
        x_e = _dot(xdt, e_p)
        s_new = s_ref[:, sl] * dec_t + x_e * bh_t
        s_out_ref[:, sl] = s_new
        r_p = jnp.where(r_row + t * reps == r_lane, 1.0, 0.0).astype(BF16)
        y = y + _dot((s_new * ch_t).astype(BF16), r_p)
    y_ref[...] = y


def ssd_state_step(state, layer, x_hp, b_h, c_h, dt_raw_s, dt_bias, a_log, d_skip, out_alias=None):
    per_b = lambda b: (b, 0, 0)
    const = lambda b: (0, 0)
    n_ssd = state.shape[0]
    in_specs = [pl.BlockSpec((None, None, SSD_HEADS, SSD_HEAD_DIM * SSD_STATE), lambda b: (layer, b, 0, 0)),
                pl.BlockSpec((None, SSD_HEADS, SSD_HEAD_DIM), per_b),
                pl.BlockSpec((None, SSD_HEADS, SSD_STATE), per_b),
                pl.BlockSpec((None, SSD_HEADS, SSD_STATE), per_b),
                pl.BlockSpec((None, 1, SSD_HEADS), per_b),
                pl.BlockSpec((1, SSD_HEADS), const),
                pl.BlockSpec((1, SSD_HEADS), const),
                pl.BlockSpec((SSD_HEADS, 1), const)]
    args = [state, x_hp, b_h, c_h, dt_raw_s, dt_bias.reshape(1, -1), a_log.reshape(1, -1), d_skip.reshape(-1, 1)]
    aliases = {}
    body = _ssd_state_step_body
    if out_alias is not None:
        in_specs.append(pl.BlockSpec(memory_space=pl.ANY))
        args.append(out_alias)
        aliases = {len(args) - 1: 0}
        body = _ssd_state_step_body_aliased
    return pl.pallas_call(
        body,
        grid=(DEC_BATCH,),
        in_specs=in_specs,
        out_specs=[pl.BlockSpec((None, None, SSD_HEADS, SSD_HEAD_DIM * SSD_STATE), lambda b: (layer, b, 0, 0)),
                   pl.BlockSpec((None, SSD_HEADS, SSD_HEAD_DIM), per_b)],
        out_shape=[jax.ShapeDtypeStruct((n_ssd, DEC_BATCH, SSD_HEADS, SSD_HEAD_DIM * SSD_STATE), F32),
                   jax.ShapeDtypeStruct((DEC_BATCH, SSD_HEADS, SSD_HEAD_DIM), F32)],
        input_output_aliases=aliases,
        compiler_params=_cparams(("parallel",)),
        name="ssd_state_step",
    )(*args)


def _ssd_state_step_body_aliased(s_ref, x_ref, b_ref, c_ref, dtr_ref, dtb_ref, alog_ref, dcol_ref, prev_ref,
                                 s_out_ref, y_ref):
    del prev_ref
    _ssd_state_step_body(s_ref, x_ref, b_ref, c_ref, dtr_ref, dtb_ref, alog_ref, dcol_ref, s_out_ref, y_ref)


def _gated_norm_body(y_ref, z_ref, w_ref, o_ref):
    gw = SSD_GROUP_WIDTH
    for g in range(SSD_GROUPS):
        sl = slice(g * gw, (g + 1) * gw)
        gt = y_ref[:, sl] * _silu(z_ref[:, sl])
        gt = gt * lax.rsqrt(jnp.mean(gt * gt, axis=1, keepdims=True) + EPS) * w_ref[:, sl]
        o_ref[:, sl] = gt.astype(o_ref.dtype)


def gated_norm_sample(y_s, z, norm_w):
    return pl.pallas_call(
        _gated_norm_body,
        grid=(1,),
        in_specs=[pl.BlockSpec((DEC_BATCH, SSD_D_INNER), lambda i: (0, 0)),
                  pl.BlockSpec((DEC_BATCH, SSD_D_INNER), lambda i: (N_PROMPT // DEC_BATCH, 0)),
                  pl.BlockSpec((1, SSD_D_INNER), lambda i: (0, 0))],
        out_specs=pl.BlockSpec((DEC_BATCH, SSD_D_INNER), lambda i: (0, 0)),
        out_shape=jax.ShapeDtypeStruct((DEC_BATCH, SSD_D_INNER), BF16),
        compiler_params=_cparams(("arbitrary",)),
        name="gated_norm_sample",
    )(y_s, z, norm_w.reshape(1, -1))


def _bf(w):
    return w.astype(BF16)


def _memory_branch(u, mem_n, w_mem_kv_i, w_in_xq, w_in_xg, cache_mem_k, cache_mem_v, layer):
    mk = matmul(mem_n, _bf(w_mem_kv_i[:, :X_WIDTH]), F32)
    mv = matmul(mem_n, _bf(w_mem_kv_i[:, X_WIDTH:]), F32)
    xq = matmul(u, _bf(w_in_xq), BF16)
    xg = matmul(u, _bf(w_in_xg), F32)
    mem_p = cross_attend_prompt(xq, xg, mk, mv)
    mem_s = cross_attend_sample(xq[N_PROMPT:].reshape(DEC_BATCH, 1, X_WIDTH),
                                xg[N_PROMPT:].reshape(DEC_BATCH, 1, X_WIDTH),
                                cache_mem_k, cache_mem_v, layer)
    mem = jnp.concatenate([mem_p, mem_s.reshape(DEC_BATCH, X_WIDTH)], axis=0)
    return mem, mk, mv


def kernel(x_prompt, x_sample, cache_ckv, cache_kpe, state_ssm, state_conv, cache_mem_k, cache_mem_v, page_table, mem_prompt, norm_w, final_norm_w, mem_norm_w, w_mem_kv, ssd_w_in, ssd_conv_w, ssd_conv_b, ssd_dt_bias, ssd_A_log, ssd_D, ssd_norm_w, ssd_w_out, mla_w_in, mla_q_norm_w, mla_kv_norm_w, mla_w_uq, mla_w_ukv, mla_w_out):
    h = jnp.concatenate([x_prompt.reshape(N_PROMPT, D_MODEL), x_sample.reshape(DEC_BATCH, D_MODEL)], axis=0)
    mem2d = mem_prompt.reshape(BATCH * N_MEM, D_MODEL)
    cache_k4 = cache_mem_k.reshape(DEPTH, DEC_BATCH, N_MEM, X_WIDTH)
    cache_v4 = cache_mem_v.reshape(DEPTH, DEC_BATCH, N_MEM, X_WIDTH)
    state4 = state_ssm.reshape(state_ssm.shape[0], DEC_BATCH, SSD_HEADS, SSD_HEAD_DIM * SSD_STATE)
    cos2, sin2 = _rope_tables()

    ckv_p, kpe_p, ckv_s, kpe_s = [], [], [], []
    ssm_p, conv_p, conv_s, mk_out, mv_out = [], [], [], [], []
    ssm_s_all = None

    for i in range(DEPTH):
        j = i // 2
        mem_n, = rmsnorm(mem2d, mem_norm_w[i], [BF16])
        u, = rmsnorm(h, norm_w[i], [BF16])
        if i % 2 == 0:
            w_in = ssd_w_in[j]
            o_xbc = SSD_D_INNER
            o_dt = o_xbc + SSD_CONV_DIM
            o_q = SSD_MIX_COLS
            z = matmul(u, _bf(w_in[:, :o_xbc]), F32)
            xbc = matmul(u, _bf(w_in[:, o_xbc:o_dt]), F32)
            dt_raw = matmul(u, _bf(w_in[:, o_dt:o_q]), F32)
            mem, mk, mv = _memory_branch(u, mem_n, w_mem_kv[i], w_in[:, o_q:o_q + X_WIDTH], w_in[:, o_q + X_WIDTH:],
                                         cache_k4, cache_v4, i)

            mix_p, ssm_pi = ssd_prompt(z, xbc, dt_raw, ssd_conv_w[j], ssd_conv_b[j], ssd_dt_bias[j], ssd_A_log[j],
                                       ssd_D[j], ssd_norm_w[j])
            ssm_p.append(ssm_pi.reshape(BATCH, SSD_HEADS, SSD_HEAD_DIM, SSD_STATE))
            conv_p.append(xbc[:N_PROMPT].reshape(BATCH, SEQ, SSD_CONV_DIM)[:, SEQ - (SSD_CONV - 1):, :])

            xconv, conv_new_t = ssd_conv_step(xbc[N_PROMPT:], jnp.transpose(state_conv[j], (1, 0, 2)),
                                              ssd_conv_w[j], ssd_conv_b[j])
            conv_s.append(jnp.transpose(conv_new_t, (1, 0, 2)))
            gn = SSD_GROUPS * SSD_STATE
            x_hp = xconv[:, :SSD_D_INNER].reshape(DEC_BATCH, SSD_HEADS, SSD_HEAD_DIM)
            b_h = jnp.repeat(xconv[:, SSD_D_INNER:SSD_D_INNER + gn].reshape(DEC_BATCH, SSD_GROUPS, SSD_STATE),
                             SSD_HEADS_PER_GROUP, axis=1)
            c_h = jnp.repeat(xconv[:, SSD_D_INNER + gn:].reshape(DEC_BATCH, SSD_GROUPS, SSD_STATE),
                             SSD_HEADS_PER_GROUP, axis=1)
            ssm_s_all, y_s = ssd_state_step(state4, j, x_hp, b_h, c_h,
                                            dt_raw[N_PROMPT:].reshape(DEC_BATCH, 1, SSD_HEADS),
                                            ssd_dt_bias[j], ssd_A_log[j], ssd_D[j], out_alias=ssm_s_all)
            mix_s = gated_norm_sample(y_s.reshape(DEC_BATCH, SSD_D_INNER), z, ssd_norm_w[j])
            w_out = ssd_w_out[j]
            n_mix = SSD_D_INNER
        else:
            w_in = mla_w_in[j]
            o_pe = MLA_Q_LORA + MLA_KV_LORA
            o_g = o_pe + MLA_ROPE
            o_q = MLA_MIX_COLS
            w_a = jnp.concatenate([w_in[:, :o_g], w_in[:, o_pe:o_g]], axis=1)
            a = matmul(u, _bf(w_a), F32)
            gate = matmul(u, _bf(w_in[:, o_g:o_q]), F32)
            mem, mk, mv = _memory_branch(u, mem_n, w_mem_kv[i], w_in[:, o_q:o_q + X_WIDTH], w_in[:, o_q + X_WIDTH:],
                                         cache_k4, cache_v4, i)

            cqn, = rmsnorm(a, mla_q_norm_w[j], [BF16], width=MLA_Q_LORA, col_block=0)
            ckv, ckv_b, kpe, kpe_dup = mla_prep(a, mla_kv_norm_w[j], cos2, sin2)
            ckv_p.append(ckv[:N_PROMPT].reshape(BATCH, SEQ, MLA_KV_LORA))
            kpe_p.append(kpe[:N_PROMPT].reshape(BATCH, SEQ, MLA_ROPE))
            ckv_s.append(ckv[N_PROMPT:].reshape(DEC_BATCH, 1, MLA_KV_LORA))
            kpe_s.append(kpe[N_PROMPT:].reshape(DEC_BATCH, 1, MLA_ROPE))

            w_uq3 = mla_w_uq[j].reshape(MLA_Q_LORA, MLA_HEADS, MLA_NOPE + MLA_ROPE)
            qn = matmul(cqn, _bf(w_uq3[:, :, :MLA_NOPE].reshape(MLA_Q_LORA, MLA_HEADS * MLA_NOPE)), BF16)
            qp_raw = matmul(cqn, _bf(w_uq3[:, :, MLA_NOPE:].reshape(MLA_Q_LORA, MLA_HEADS * MLA_ROPE)), F32)
            qp = rope_heads(qp_raw, cos2, sin2, BF16)
            w_ukv_b = _bf(mla_w_ukv[j])
            kvup = matmul(ckv_b, w_ukv_b, BF16)
            mix_p = mla_flash_prompt(qn, qp, kvup, kpe_dup, gate)

            q_lat = mla_absorb_q(qn[N_PROMPT:], w_ukv_b).reshape(DEC_BATCH, MLA_HEADS, MLA_KV_LORA)
            o_lat = mla_paged_sample(q_lat, qp[N_PROMPT:].reshape(DEC_BATCH, MLA_HEADS, MLA_ROPE),
                                     ckv_s[-1], kpe_s[-1], cache_ckv, cache_kpe, page_table, j)
            mix_s = mla_merge_sample(o_lat.reshape(DEC_BATCH, MLA_HEADS * MLA_KV_LORA), w_ukv_b, gate[N_PROMPT:])
            w_out = mla_w_out[j]
            n_mix = MLA_WIDTH

        mk_out.append(mk.reshape(BATCH, N_MEM, X_HEADS, X_HEAD_DIM))
        mv_out.append(mv.reshape(BATCH, N_MEM, X_HEADS, X_HEAD_DIM))
        mix = jnp.concatenate([mix_p, mix_s], axis=0)
        h = matmul(mix, _bf(w_out[:n_mix]), F32, res=h)
        h = matmul(mem, _bf(w_out[n_mix:]), F32, res=h)

    y, = rmsnorm(h, final_norm_w, [F32])
    y_prompt = y[:N_PROMPT].reshape(BATCH, SEQ, D_MODEL)
    y_sample = y[N_PROMPT:].reshape(DEC_BATCH, 1, D_MODEL)
    ssm_sample = ssm_s_all.reshape(ssm_s_all.shape[0], DEC_BATCH, SSD_HEADS, SSD_HEAD_DIM, SSD_STATE)
    return (y_prompt, y_sample, jnp.stack(ckv_p), jnp.stack(kpe_p), jnp.stack(ckv_s), jnp.stack(kpe_s),
            jnp.stack(ssm_p), ssm_sample, jnp.stack(conv_p), jnp.stack(conv_s),
            jnp.stack(mk_out), jnp.stack(mv_out))
```

```python
import functools
import math

import jax
import jax.numpy as jnp
from jax import lax
from jax.experimental import pallas as pl
from jax.experimental.pallas import tpu as pltpu

F32 = jnp.float32
BF16 = jnp.bfloat16

D_MODEL = 4096
BATCH = 4
SEQ = 2048
DEPTH = 4
DEC_BATCH = 128
PAST_LEN = 8192
PAGE_SIZE = 128
N_PAGES = PAST_LEN // PAGE_SIZE
EPS = 1e-6

N_MEM = 256
X_HEADS = 4
X_HEAD_DIM = 768
X_WIDTH = X_HEADS * X_HEAD_DIM
X_SCALE = X_HEAD_DIM ** -0.5

SSD_D_INNER = 2 * D_MODEL
SSD_HEAD_DIM = 64
SSD_HEADS = SSD_D_INNER // SSD_HEAD_DIM
SSD_GROUPS = 8
SSD_HEADS_PER_GROUP = SSD_HEADS // SSD_GROUPS
SSD_GROUP_WIDTH = SSD_D_INNER // SSD_GROUPS
SSD_STATE = 128
SSD_CONV = 4
SSD_CONV_DIM = SSD_D_INNER + 2 * SSD_GROUPS * SSD_STATE
SSD_CHUNK = 128
SSD_MIX_COLS = SSD_D_INNER + SSD_CONV_DIM + SSD_HEADS

MLA_HEADS = 32
MLA_Q_LORA = 1024
MLA_KV_LORA = 512
MLA_NOPE = 128
MLA_ROPE = 64
MLA_V = 128
MLA_WIDTH = MLA_HEADS * MLA_V
MLA_MIX_COLS = MLA_Q_LORA + MLA_KV_LORA + MLA_ROPE + MLA_WIDTH
MLA_SCALE = (MLA_NOPE + MLA_ROPE) ** -0.5
MLA_SCALE_LOG2 = MLA_SCALE * math.log2(math.e)
ROPE_THETA = 10000.0

N_PROMPT = BATCH * SEQ
N_TOK = N_PROMPT + DEC_BATCH

V7X_LANES = 128
V7X_SUBLANES = 8
V7X_VMEM_LIMIT_BYTES = 56 * 1024 * 1024
V7X_VMEM_TILE_BUDGET_BYTES = 44 * 1024 * 1024

MM_COL_ALIGN = 1024
SSD_IN_OFF_Z = 0
SSD_IN_OFF_XBC = SSD_D_INNER
SSD_IN_OFF_DT = SSD_D_INNER + SSD_CONV_DIM
MLA_A_COLS = MLA_Q_LORA + MLA_KV_LORA + V7X_LANES
MLA_IN_OFF_GATE = -(-MLA_A_COLS // MM_COL_ALIGN) * MM_COL_ALIGN
MLA_IN_OFF_XQ = MLA_IN_OFF_GATE + MLA_WIDTH
MLA_IN_OFF_XG = MLA_IN_OFF_XQ + X_WIDTH
MLA_IN_COLS = MLA_IN_OFF_XG + X_WIDTH


def _cparams(semantics):
    return pltpu.CompilerParams(dimension_semantics=semantics, vmem_limit_bytes=V7X_VMEM_LIMIT_BYTES)


def _sigmoid(x):
    return 1.0 / (1.0 + jnp.exp(-x))


def _silu(x):
    return x * _sigmoid(x)


def _softplus(x):
    return jnp.maximum(x, 0.0) + jnp.log(1.0 + jnp.exp(-jnp.abs(x)))


def _dot(a, b):
    return jnp.dot(a, b, preferred_element_type=F32)


def _dot_nt(a, b):
    return lax.dot_general(a, b, (((1,), (1,)), ((), ())), preferred_element_type=F32)


def _dot_tn(a, b):
    return lax.dot_general(a, b, (((0,), (0,)), ((), ())), preferred_element_type=F32)


def _split3(x):
    hi = x.astype(BF16)
    r1 = x - hi.astype(F32)
    mid = r1.astype(BF16)
    lo = (r1 - mid.astype(F32)).astype(BF16)
    return hi, mid, lo


def _dot3(x, e):
    hi, mid, lo = _split3(x)
    return _dot(hi, e) + _dot(mid, e) + _dot(lo, e)


def _dot3_left(e, x):
    hi, mid, lo = _split3(x)
    return _dot(e, hi) + _dot(e, mid) + _dot(e, lo)


def _row_block(m, target):
    best = None
    for d in range(16, min(m, target) + 1, 16):
        if m % d == 0:
            best = d
    assert best is not None, (m, target)
    return best


def _rmsnorm_body(x_ref, w_ref, *o_refs):
    x = x_ref[...]
    y = x * lax.rsqrt(jnp.mean(x * x, axis=-1, keepdims=True) + EPS) * w_ref[...]
    for o_ref in o_refs:
        o_ref[...] = y.astype(o_ref.dtype)


def rmsnorm(x, w, out_dtypes, width=None, col_block=0, row_start=0, rows=None):
    m = rows or x.shape[0]
    d = width or x.shape[1]
    bm = _row_block(m, max(16, (2 * 1024 * 1024) // (d * 4)))
    assert row_start % bm == 0
    r0 = row_start // bm
    outs = pl.pallas_call(
        _rmsnorm_body,
        grid=(m // bm,),
        in_specs=[pl.BlockSpec((bm, d), lambda i: (i + r0, col_block)),
                  pl.BlockSpec((1, d), lambda i: (0, 0))],
        out_specs=[pl.BlockSpec((bm, d), lambda i: (i, 0)) for _ in out_dtypes],
        out_shape=[jax.ShapeDtypeStruct((m, d), dt) for dt in out_dtypes],
        compiler_params=_cparams(("parallel",)),
        name="rmsnorm",
    )(x, w.reshape(1, d))
    return outs


def _mm_body(*refs, nks, has_res, out_scale):
    nsrc = len(nks)
    x_refs = refs[:nsrc]
    w_ref = refs[nsrc]
    r_ref = refs[nsrc + 1] if has_res else None
    o_ref = refs[nsrc + 1 + int(has_res)]
    scratch = refs[nsrc + 2 + int(has_res):]
    nk = sum(nks)

    def finish(acc):
        if out_scale is not None:
            acc = acc * out_scale
        if has_res:
            acc = acc + r_ref[...]
        o_ref[...] = acc.astype(o_ref.dtype)

    if nk == 1:
        finish(_dot(x_refs[0][...], w_ref[...]))
        return
    acc_ref, = scratch
    k = pl.program_id(2)

    @pl.when(k == 0)
    def _():
        acc_ref[...] = jnp.zeros_like(acc_ref)

    start = 0
    for x_ref, nk_s in zip(x_refs, nks):
        if nsrc == 1:
            acc_ref[...] += _dot(x_ref[...], w_ref[...])
        else:
            @pl.when((k >= start) & (k < start + nk_s))
            def _(x_ref=x_ref):
                acc_ref[...] += _dot(x_ref[...], w_ref[...])
        start += nk_s

    @pl.when(k == nk - 1)
    def _():
        finish(acc_ref[...])


MM_K_BLOCK = 1024


def _mm_blocks(m, ks, n, col0, out_bytes, has_res):
    single = len(ks) == 1 and ks[0] <= 4096
    bk = ks[0] if single else MM_K_BLOCK
    bn_cands = [c for c in (1024, 768, 512, 384, 256, 128) if n % c == 0 and col0 % c == 0]
    if col0 == 0 and n <= 2048 and n not in bn_cands:
        bn_cands = [n] + bn_cands
    best = None
    for bn in bn_cands:
        for bm_target in (1040, 640, 520, 320, 208, 128):
            bm = _row_block(m, bm_target)
            need = 2 * (len(ks) * bm * bk * 2 + bk * bn * 2 + bm * bn * out_bytes)
            if not single:
                need += bm * bn * 4
            if has_res:
                need += 2 * bm * bn * 4
            if need <= V7X_VMEM_TILE_BUDGET_BYTES:
                cand = (bm * bn, bm, bn)
                if best is None or cand > best:
                    best = cand
                break
    assert best is not None, (m, ks, n)
    return best[1], best[2], bk


def matmul(xs, w, layer, n, out_dtype, col0=0, row0=0, res=None, out_scale=None):
    if not isinstance(xs, (list, tuple)):
        xs = [xs]
    m = xs[0].shape[0]
    ks = [x.shape[1] for x in xs]
    has_res = res is not None
    bm, bn, bk = _mm_blocks(m, ks, n, col0, jnp.dtype(out_dtype).itemsize, has_res)
    nks = [k // bk for k in ks]
    assert all(k % bk == 0 for k in ks) and row0 % bk == 0 and col0 % bn == 0
    nk = sum(nks)
    in_specs = []
    start = 0
    for nk_s in nks:
        in_specs.append(pl.BlockSpec(
            (bm, bk), lambda j, i, kk, start=start, nk_s=nk_s: (i, jnp.clip(kk - start, 0, nk_s - 1))))
        start += nk_s
    in_specs.append(pl.BlockSpec((None, bk, bn), lambda j, i, kk: (layer, row0 // bk + kk, col0 // bn + j)))
    args = list(xs) + [w]
    if has_res:
        in_specs.append(pl.BlockSpec((bm, bn), lambda j, i, kk: (i, j)))
        args.append(res)
    return pl.pallas_call(
        functools.partial(_mm_body, nks=tuple(nks), has_res=has_res, out_scale=out_scale),
        grid=(n // bn, m // bm, nk),
        in_specs=in_specs,
        out_specs=pl.BlockSpec((bm, bn), lambda j, i, kk: (i, j)),
        out_shape=jax.ShapeDtypeStruct((m, n), out_dtype),
        scratch_shapes=[pltpu.VMEM((bm, bn), F32)] if nk > 1 else [],
        compiler_params=_cparams(("parallel", "parallel", "arbitrary")),
        name="matmul",
    )(*args)


def _rope_tables():
    half = MLA_ROPE // 2
    inv = ROPE_THETA ** (-jnp.arange(half, dtype=F32) / half)
    pos = jnp.concatenate([jnp.tile(jnp.arange(SEQ), BATCH),
                           jnp.full((DEC_BATCH,), PAST_LEN, jnp.int32)]).astype(F32)
    ang = pos[:, None] * inv[None, :]
    cos = jnp.cos(ang)
    sin = jnp.sin(ang)
    cos2 = jnp.tile(cos, (1, V7X_LANES // half))
    sin2 = jnp.tile(jnp.concatenate([-sin, sin], axis=1), (1, V7X_LANES // MLA_ROPE))
    return cos2, sin2


def _rotate_pairs(x, cos2, sin2, first_half):
    swapped = jnp.where(first_half, pltpu.roll(x, V7X_LANES - MLA_ROPE // 2, 1), pltpu.roll(x, MLA_ROPE // 2, 1))
    return x * cos2 + swapped * sin2


def _rope_body(x_ref, c_ref, s_ref, o_ref, *, out_scale):
    cos2 = c_ref[...]
    sin2 = s_ref[...]
    lane = lax.broadcasted_iota(jnp.int32, cos2.shape, 1)
    first_half = (lane % MLA_ROPE) < (MLA_ROPE // 2)
    for t in range(x_ref.shape[1] // V7X_LANES):
        sl = slice(t * V7X_LANES, (t + 1) * V7X_LANES)
        o_ref[:, sl] = (_rotate_pairs(x_ref[:, sl], cos2, sin2, first_half) * out_scale).astype(o_ref.dtype)


def rope_heads(x, cos2, sin2, out_dtype, out_scale):
    m, n = x.shape
    bm = _row_block(m, 320)
    return pl.pallas_call(
        functools.partial(_rope_body, out_scale=out_scale),
        grid=(m // bm,),
        in_specs=[pl.BlockSpec((bm, n), lambda i: (i, 0)),
                  pl.BlockSpec((bm, V7X_LANES), lambda i: (i, 0)),
                  pl.BlockSpec((bm, V7X_LANES), lambda i: (i, 0))],
        out_specs=pl.BlockSpec((bm, n), lambda i: (i, 0)),
        out_shape=jax.ShapeDtypeStruct((m, n), out_dtype),
        compiler_params=_cparams(("parallel",)),
        name="rope_heads",
    )(x, cos2, sin2)


def _mla_prep_body(ckv_in_ref, kpe_in_ref, w_ref, c_ref, s_ref, ckv_ref, ckvb_ref, kpe_ref, kpeb_ref):
    x = ckv_in_ref[...]
    y = x * lax.rsqrt(jnp.mean(x * x, axis=-1, keepdims=True) + EPS) * w_ref[...]
    ckv_ref[...] = y
    ckvb_ref[...] = y.astype(BF16)
    cos2 = c_ref[...]
    lane = lax.broadcasted_iota(jnp.int32, cos2.shape, 1)
    first_half = (lane % MLA_ROPE) < (MLA_ROPE // 2)
    r = _rotate_pairs(kpe_in_ref[...], cos2, s_ref[...], first_half)
    kpe_ref[...] = r[:, :MLA_ROPE]
    kpeb_ref[...] = r.astype(BF16)


def mla_prep(a, kv_norm_w, cos2, sin2):
    m = a.shape[0]
    bm = _row_block(m, 640)
    row = lambda i: (i, 0)
    return pl.pallas_call(
        _mla_prep_body,
        grid=(m // bm,),
        in_specs=[pl.BlockSpec((bm, MLA_KV_LORA), lambda i: (i, MLA_Q_LORA // MLA_KV_LORA)),
                  pl.BlockSpec((bm, V7X_LANES), lambda i: (i, (MLA_Q_LORA + MLA_KV_LORA) // V7X_LANES)),
                  pl.BlockSpec((1, MLA_KV_LORA), lambda i: (0, 0)),
                  pl.BlockSpec((bm, V7X_LANES), row),
                  pl.BlockSpec((bm, V7X_LANES), row)],
        out_specs=[pl.BlockSpec((bm, MLA_KV_LORA), row), pl.BlockSpec((bm, MLA_KV_LORA), row),
                   pl.BlockSpec((bm, MLA_ROPE), row), pl.BlockSpec((bm, V7X_LANES), row)],
        out_shape=[jax.ShapeDtypeStruct((m, MLA_KV_LORA), F32), jax.ShapeDtypeStruct((m, MLA_KV_LORA), BF16),
                   jax.ShapeDtypeStruct((m, MLA_ROPE), F32), jax.ShapeDtypeStruct((m, V7X_LANES), BF16)],
        compiler_params=_cparams(("parallel",)),
        name="mla_prep",
    )(a, a, kv_norm_w.reshape(1, MLA_KV_LORA), cos2, sin2)


FLASH_T = 512


def _flash_body(qn_ref, qp_ref, kv_ref, kpe_ref, g_ref, o_ref):
    qi = pl.program_id(2)
    t = FLASH_T
    lane = lax.broadcasted_iota(jnp.int32, (t, V7X_LANES), 1)
    qp = qp_ref[...]
    row = lax.broadcasted_iota(jnp.int32, (t, t), 0)
    col = lax.broadcasted_iota(jnp.int32, (t, t), 1)
    qs = []
    for h in range(2):
        own = (lane < MLA_ROPE) if h == 0 else (lane >= MLA_ROPE)
        qs.append(jnp.concatenate([qn_ref[:, h * MLA_NOPE:(h + 1) * MLA_NOPE],
                                   jnp.where(own, qp, jnp.zeros_like(qp))], axis=1))

    def step(j, carry, masked):
        off = pl.multiple_of(j * t, t)
        kpe = kpe_ref[pl.ds(off, t), :]
        out = []
        for h in range(2):
            m_prev, l_prev, acc = carry[h]
            kbase = h * (MLA_NOPE + MLA_V)
            k = jnp.concatenate([kv_ref[pl.ds(off, t), kbase:kbase + MLA_NOPE], kpe], axis=1)
            v = kv_ref[pl.ds(off, t), kbase + MLA_NOPE:kbase + MLA_NOPE + MLA_V]
            s = _dot_nt(qs[h], k)
            if masked:
                s = jnp.where(col <= row, s, -jnp.inf)
            m_new = jnp.maximum(m_prev, jnp.max(s, axis=1, keepdims=True))
            alpha = jnp.exp2(m_prev - m_new)
            p = jnp.exp2(s - m_new)
            l_new = l_prev * alpha + jnp.sum(p, axis=1, keepdims=True)
            out.append((m_new, l_new, acc * alpha + _dot(p.astype(BF16), v)))
        return tuple(out)

    init_h = (jnp.full((t, 1), -jnp.inf, F32), jnp.zeros((t, 1), F32), jnp.zeros((t, MLA_V), F32))
    carry = lax.fori_loop(0, qi, functools.partial(step, masked=False), (init_h, init_h))
    carry = step(qi, carry, True)
    for h in range(2):
        _, l_fin, acc = carry[h]
        sl = slice(h * MLA_V, (h + 1) * MLA_V)
        o_ref[:, sl] = (acc / l_fin * _silu(g_ref[:, sl])).astype(o_ref.dtype)


def mla_flash_prompt(qn, qp, kvup, kpe_dup, gate):
    nq = SEQ // FLASH_T
    row_blk = lambda b, hp, qi: (b * nq + qi, hp)
    return pl.pallas_call(
        _flash_body,
        grid=(BATCH, MLA_HEADS // 2, nq),
        in_specs=[pl.BlockSpec((FLASH_T, 2 * MLA_NOPE), row_blk),
                  pl.BlockSpec((FLASH_T, V7X_LANES), row_blk),
                  pl.BlockSpec((SEQ, 2 * (MLA_NOPE + MLA_V)), lambda b, hp, qi: (b, hp)),
                  pl.BlockSpec((SEQ, V7X_LANES), lambda b, hp, qi: (b, 0)),
                  pl.BlockSpec((FLASH_T, 2 * MLA_V), row_blk)],
        out_specs=pl.BlockSpec((FLASH_T, 2 * MLA_V), row_blk),
        out_shape=jax.ShapeDtypeStruct((N_TOK, MLA_WIDTH), BF16),
        compiler_params=_cparams(("parallel", "parallel", "arbitrary")),
        name="mla_flash_prompt",
    )(qn, qp, kvup, kpe_dup, gate)


PAGES_PER_STEP = 8


def _paged_body(pt_ref, ql_ref, qp_ref, cn_ref, kn_ref, *rest):
    del pt_ref
    pps = PAGES_PER_STEP
    ck_refs = rest[:pps]
    kp_refs = rest[pps:2 * pps]
    o_ref = rest[2 * pps]
    m_ref, l_ref, acc_ref = rest[2 * pps + 1:]
    step = pl.program_id(1)
    ql = ql_ref[...]
    qp = qp_ref[...]

    @pl.when(step == 0)
    def _():
        cn = cn_ref[...]
        m_ref[...] = (jnp.sum(ql.astype(F32) * cn, axis=1, keepdims=True)
                      + jnp.sum(qp.astype(F32) * kn_ref[...], axis=1, keepdims=True))
        l_ref[...] = jnp.ones_like(l_ref)
        acc_ref[...] = jnp.broadcast_to(cn, acc_ref.shape)

    kcs = [r[...].astype(BF16) for r in ck_refs]
    s = jnp.concatenate([_dot_nt(ql, kc) + _dot(qp, kr[...].astype(BF16))
                         for kc, kr in zip(kcs, kp_refs)], axis=1)
    m_prev = m_ref[...]
    m_new = jnp.maximum(m_prev, jnp.max(s, axis=1, keepdims=True))
    alpha = jnp.exp2(m_prev - m_new)
    p = jnp.exp2(s - m_new)
    l_ref[...] = l_ref[...] * alpha + jnp.sum(p, axis=1, keepdims=True)
    pb = p.astype(BF16)
    acc = acc_ref[...] * alpha
    for i, kc in enumerate(kcs):
        acc = acc + _dot(pb[:, i * PAGE_SIZE:(i + 1) * PAGE_SIZE], kc)
    acc_ref[...] = acc
    m_ref[...] = m_new

    @pl.when(step == pl.num_programs(1) - 1)
    def _():
        o_ref[...] = acc_ref[...] / l_ref[...]


def mla_paged_sample(q_lat, q_pe, ckv_new, kpe_new, cache_ckv, cache_kpe_t, page_table, layer):
    pps = PAGES_PER_STEP
    nsteps = N_PAGES // pps
    per_b = lambda b, s, pt: (b, 0, 0)

    def page_map(i):
        return lambda b, s, pt: (layer, pt[b * N_PAGES + s * pps + i], 0, 0)

    in_specs = [pl.BlockSpec((None, MLA_HEADS, MLA_KV_LORA), per_b),
                pl.BlockSpec((None, MLA_HEADS, MLA_ROPE), per_b),
                pl.BlockSpec((None, 1, MLA_KV_LORA), per_b),
                pl.BlockSpec((None, 1, MLA_ROPE), per_b)]
    in_specs += [pl.BlockSpec((None, None, PAGE_SIZE, MLA_KV_LORA), page_map(i)) for i in range(pps)]
    in_specs += [pl.BlockSpec((None, None, MLA_ROPE, PAGE_SIZE), page_map(i)) for i in range(pps)]
    grid_spec = pltpu.PrefetchScalarGridSpec(
        num_scalar_prefetch=1,
        grid=(DEC_BATCH, nsteps),
        in_specs=in_specs,
        out_specs=pl.BlockSpec((None, MLA_HEADS, MLA_KV_LORA), per_b),
        scratch_shapes=[pltpu.VMEM((MLA_HEADS, 1), F32), pltpu.VMEM((MLA_HEADS, 1), F32),
                        pltpu.VMEM((MLA_HEADS, MLA_KV_LORA), F32)],
    )
    return pl.pallas_call(
        _paged_body,
        grid_spec=grid_spec,
        out_shape=jax.ShapeDtypeStruct((DEC_BATCH, MLA_HEADS, MLA_KV_LORA), F32),
        compiler_params=_cparams(("parallel", "arbitrary")),
        name="mla_paged_sample",
    )(page_table.reshape(-1), q_lat, q_pe, ckv_new, kpe_new,
      *([cache_ckv] * pps), *([cache_kpe_t] * pps))


def _absorb_body(x_ref, w_ref, o_ref):
    o_ref[...] = _dot_nt(x_ref[...], w_ref[...]).astype(o_ref.dtype)


def mla_absorb_q(q_nope, w_ukv, layer):
    m = q_nope.shape[0]
    return pl.pallas_call(
        _absorb_body,
        grid=(MLA_HEADS,),
        in_specs=[pl.BlockSpec((m, MLA_NOPE), lambda h: (0, h)),
                  pl.BlockSpec((None, MLA_KV_LORA, MLA_NOPE), lambda h: (layer, 0, 2 * h))],
        out_specs=pl.BlockSpec((m, MLA_KV_LORA), lambda h: (0, h)),
        out_shape=jax.ShapeDtypeStruct((m, MLA_HEADS * MLA_KV_LORA), BF16),
        compiler_params=_cparams(("parallel",)),
        name="mla_absorb_q",
    )(q_nope, w_ukv)


def _merge_body(x_ref, w_ref, g_ref, mix_ref, o_ref):
    del mix_ref
    o = _dot(x_ref[...].astype(BF16), w_ref[...])
    o_ref[...] = (o * _silu(g_ref[...])).astype(o_ref.dtype)


def mla_merge_sample(o_lat, w_ukv, layer, gate, mix):
    m = o_lat.shape[0]
    s_blk = N_PROMPT // m
    return pl.pallas_call(
        _merge_body,
        grid=(MLA_HEADS,),
        in_specs=[pl.BlockSpec((m, MLA_KV_LORA), lambda h: (0, h)),
                  pl.BlockSpec((None, MLA_KV_LORA, MLA_V), lambda h: (layer, 0, 2 * h + 1)),
                  pl.BlockSpec((m, MLA_V), lambda h: (s_blk, h)),
                  pl.BlockSpec(memory_space=pl.ANY)],
        out_specs=pl.BlockSpec((m, MLA_V), lambda h: (s_blk, h)),
        out_shape=jax.ShapeDtypeStruct(mix.shape, mix.dtype),
        input_output_aliases={3: 0},
        compiler_params=_cparams(("parallel",)),
        name="mla_merge_sample",
    )(o_lat, w_ukv, gate, mix)


CROSS_TQ = 256


def _cross_prompt_body(q_ref, g_ref, k_ref, v_ref, o_ref):
    for h in range(X_HEADS):
        sl = slice(h * X_HEAD_DIM, (h + 1) * X_HEAD_DIM)
        s = _dot_nt(q_ref[:, sl], k_ref[:, sl].astype(BF16)) * X_SCALE
        m = jnp.max(s, axis=1, keepdims=True)
        p = jnp.exp(s - m)
        p = p / jnp.sum(p, axis=1, keepdims=True)
        o = _dot(p.astype(BF16), v_ref[:, sl].astype(BF16))
        o_ref[:, sl] = (o * _silu(g_ref[:, sl])).astype(o_ref.dtype)


def cross_attend_prompt(xq, xg, mk, mv):
    nq = SEQ // CROSS_TQ
    row_blk = lambda b, qi: (b * nq + qi, 0)
    return pl.pallas_call(
        _cross_prompt_body,
        grid=(BATCH, nq),
        in_specs=[pl.BlockSpec((CROSS_TQ, X_WIDTH), row_blk),
                  pl.BlockSpec((CROSS_TQ, X_WIDTH), row_blk),
                  pl.BlockSpec((N_MEM, X_WIDTH), lambda b, qi: (b, 0)),
                  pl.BlockSpec((N_MEM, X_WIDTH), lambda b, qi: (b, 0))],
        out_specs=pl.BlockSpec((CROSS_TQ, X_WIDTH), row_blk),
        out_shape=jax.ShapeDtypeStruct((N_PROMPT, X_WIDTH), BF16),
        compiler_params=_cparams(("parallel", "parallel")),
        name="cross_attend_prompt",
    )(xq, xg, mk, mv)


def _cross_sample_body(q_ref, g_ref, k_ref, v_ref, o_ref):
    s = jnp.sum(k_ref[...] * q_ref[...][None], axis=-1, keepdims=True) * X_SCALE
    p = jnp.exp(s - jnp.max(s, axis=0, keepdims=True))
    o = jnp.sum(p * v_ref[...], axis=0) / jnp.sum(p, axis=0)
    o_ref[...] = (o * _silu(g_ref[...])).astype(o_ref.dtype)


def cross_attend_sample(xq_s, xg_s, cache_k, cache_v, layer):
    per_b = lambda b: (b, 0, 0)
    cache_blk = lambda b: (layer, b, 0, 0, 0)
    return pl.pallas_call(
        _cross_sample_body,
        grid=(DEC_BATCH,),
        in_specs=[pl.BlockSpec((None, X_HEADS, X_HEAD_DIM), per_b),
                  pl.BlockSpec((None, X_HEADS, X_HEAD_DIM), per_b),
                  pl.BlockSpec((None, None, N_MEM, X_HEADS, X_HEAD_DIM), cache_blk),
                  pl.BlockSpec((None, None, N_MEM, X_HEADS, X_HEAD_DIM), cache_blk)],
        out_specs=pl.BlockSpec((None, X_HEADS, X_HEAD_DIM), per_b),
        out_shape=jax.ShapeDtypeStruct((DEC_BATCH, X_HEADS, X_HEAD_DIM), BF16),
        compiler_params=_cparams(("parallel",)),
        name="cross_attend_sample",
    )(xq_s, xg_s, cache_k, cache_v)


SSD_CARRY_ROWS = 8


def _ssd_prompt_body(z_ref, xbc_ref, dt_ref, cw_ref, cb_ref, dtb_ref, alog_ref, dexp_ref, nw_ref,
                     o_ref, ssm_ref, xs_ref, ht_ref, at_ref):
    c = pl.program_id(1)
    q = SSD_CHUNK
    gw = SSD_GROUP_WIDTH
    hpg = SSD_HEADS_PER_GROUP
    base = SSD_CARRY_ROWS

    @pl.when(c == 0)
    def _():
        ht_ref[...] = jnp.zeros_like(ht_ref)
        xs_ref[0:base, :] = jnp.zeros((base, SSD_CONV_DIM), F32)

    xs_ref[base:base + q, :] = xbc_ref[...]

    def conv(off, width):
        cols = pl.ds(off, width)
        acc = cb_ref[:, cols]
        for tap in range(SSD_CONV):
            lo = base - (SSD_CONV - 1) + tap
            acc = acc + cw_ref[tap:tap + 1, cols] * xs_ref[lo:lo + q, cols]
        return _silu(acc)

    dt = _softplus(dt_ref[...] + dtb_ref[...])
    da = dt * (-jnp.exp(alog_ref[...]))
    ti = lax.broadcasted_iota(jnp.int32, (q, q), 0)
    tj = lax.broadcasted_iota(jnp.int32, (q, q), 1)
    causal = tj <= ti
    tri = jnp.where(causal, 1.0, 0.0).astype(BF16)
    a = _dot3_left(tri, da)
    at_ref[...] = a.T

    head_row = lax.broadcasted_iota(jnp.int32, (SSD_HEADS, gw), 0)
    exp_lane = lax.broadcasted_iota(jnp.int32, (SSD_HEADS, gw), 1) // SSD_HEAD_DIM
    col_row = lax.broadcasted_iota(jnp.int32, (SSD_HEADS, hpg * q), 0)
    col_lane = lax.broadcasted_iota(jnp.int32, (SSD_HEADS, hpg * q), 1) // q
    pair_lane = lax.broadcasted_iota(jnp.int32, (q, 2 * SSD_HEAD_DIM), 1)

    def group(g, _):
        ox = pl.multiple_of(g * gw, gw)
        ob = pl.multiple_of(SSD_D_INNER + g * SSD_STATE, SSD_STATE)
        oc = pl.multiple_of(SSD_D_INNER + SSD_GROUPS * SSD_STATE + g * SSD_STATE, SSD_STATE)
        xg = conv(ox, gw)
        bg = conv(ob, SSD_STATE)
        cg = conv(oc, SSD_STATE)
        bgb = bg.astype(BF16)
        cgb = cg.astype(BF16)

        e_hp = jnp.where(head_row == g * hpg + exp_lane, 1.0, 0.0).astype(BF16)
        dt_e = _dot3(dt, e_hp)
        a_e = _dot3(a, e_hp)
        a_last = a_e[q - 1:q, :]
        xdt = xg * dt_e
        xw = xdt * jnp.exp(a_last - a_e)
        xdtb = xdt.astype(BF16)

        cb = _dot_nt(cgb, bgb)
        e_col = jnp.where(col_row == g * hpg + col_lane, 1.0, 0.0).astype(BF16)
        a_col = _dot3(a, e_col)

        pieces = []
        for kp in range(hpg // 2):
            xpair = xdtb[:, kp * 2 * SSD_HEAD_DIM:(kp + 1) * 2 * SSD_HEAD_DIM]
            ypair = jnp.zeros((q, 2 * SSD_HEAD_DIM), F32)
            for hh in range(2):
                k = 2 * kp + hh
                a_row = at_ref[pl.ds(g * hpg + k, 1), :]
                seg = a_col[:, k * q:(k + 1) * q] - a_row
                decay = jnp.exp(jnp.where(causal, seg, -jnp.inf))
                own = (pair_lane < SSD_HEAD_DIM) if hh == 0 else (pair_lane >= SSD_HEAD_DIM)
                ypair = ypair + _dot((cb * decay).astype(BF16), jnp.where(own, xpair, jnp.zeros_like(xpair)))
            pieces.append(ypair)
        y_intra = jnp.concatenate(pieces, axis=1)

        ht_g = ht_ref[:, pl.ds(ox, gw)]
        y = y_intra + _dot(cgb, ht_g.astype(BF16)) * jnp.exp(a_e) + xg * dexp_ref[:, pl.ds(ox, gw)]
        zg = z_ref[:, pl.ds(ox, gw)]
        gt = y * _silu(zg)
        gt = gt * lax.rsqrt(jnp.mean(gt * gt, axis=1, keepdims=True) + EPS) * nw_ref[:, pl.ds(ox, gw)]
        o_ref[:, pl.ds(ox, gw)] = gt.astype(o_ref.dtype)

        ht_ref[:, pl.ds(ox, gw)] = ht_g * jnp.exp(a_last) + _dot(bg.T.astype(BF16), xw.astype(BF16))
        return 0

    lax.fori_loop(0, SSD_GROUPS, group, 0)
    xs_ref[0:base, :] = xs_ref[q:q + base, :]

    @pl.when(c == pl.num_programs(1) - 1)
    def _():
        for g in range(SSD_GROUPS):
            ssm_ref[g * gw:(g + 1) * gw, :] = ht_ref[:, g * gw:(g + 1) * gw].T


def ssd_prompt(z, xbc, dt_raw, conv_w, conv_b, dt_bias, a_log, d_exp, norm_w):
    nc = SEQ // SSD_CHUNK
    row_blk = lambda b, c: (b * nc + c, 0)
    const = lambda b, c: (0, 0)
    return pl.pallas_call(
        _ssd_prompt_body,
        grid=(BATCH, nc),
        in_specs=[pl.BlockSpec((SSD_CHUNK, SSD_D_INNER), row_blk),
                  pl.BlockSpec((SSD_CHUNK, SSD_CONV_DIM), row_blk),
                  pl.BlockSpec((SSD_CHUNK, SSD_HEADS), row_blk),
                  pl.BlockSpec((SSD_CONV, SSD_CONV_DIM), const),
                  pl.BlockSpec((1, SSD_CONV_DIM), const),
                  pl.BlockSpec((1, SSD_HEADS), const),
                  pl.BlockSpec((1, SSD_HEADS), const),
                  pl.BlockSpec((1, SSD_D_INNER), const),
                  pl.BlockSpec((1, SSD_D_INNER), const)],
        out_specs=[pl.BlockSpec((SSD_CHUNK, SSD_D_INNER), row_blk),
                   pl.BlockSpec((None, SSD_D_INNER, SSD_STATE), lambda b, c: (b, 0, 0))],
        out_shape=[jax.ShapeDtypeStruct((N_TOK, SSD_D_INNER), BF16),
                   jax.ShapeDtypeStruct((BATCH, SSD_D_INNER, SSD_STATE), F32)],
        scratch_shapes=[pltpu.VMEM((SSD_CARRY_ROWS + SSD_CHUNK, SSD_CONV_DIM), F32),
                        pltpu.VMEM((SSD_STATE, SSD_D_INNER), F32),
                        pltpu.VMEM((SSD_HEADS, SSD_CHUNK), F32)],
        compiler_params=_cparams(("parallel", "arbitrary")),
        name="ssd_prompt",
    )(z, xbc, dt_raw, conv_w, conv_b.reshape(1, -1), dt_bias.reshape(1, -1), a_log.reshape(1, -1),
      d_exp, norm_w.reshape(1, -1))


def _ssd_sample_prep_body(x_ref, st_ref, cw_ref, cb_ref, dtr_ref, dtb_ref, alog_ref, dexp_ref,
                          conv_ref, st_out_ref, xdt_ref, xd_ref, dec_ref):
    j = pl.program_id(0)
    gw = SSD_GROUP_WIDTH
    x = x_ref[...]
    acc = cb_ref[...] + cw_ref[SSD_CONV - 1:SSD_CONV, :] * x
    for tap in range(SSD_CONV - 1):
        acc = acc + cw_ref[tap:tap + 1, :] * st_ref[tap]
    xc = _silu(acc)
    conv_ref[...] = xc
    for tap in range(SSD_CONV - 2):
        st_out_ref[tap] = st_ref[tap + 1]
    st_out_ref[SSD_CONV - 2] = x

    dt = _softplus(dtr_ref[...] + dtb_ref[...])
    dec_ref[...] = jnp.exp(dt * (-jnp.exp(alog_ref[...])))

    @pl.when(j < SSD_GROUPS)
    def _():
        head_row = lax.broadcasted_iota(jnp.int32, (SSD_HEADS, gw), 0)
        exp_lane = lax.broadcasted_iota(jnp.int32, (SSD_HEADS, gw), 1) // SSD_HEAD_DIM
        e_hp = jnp.where(head_row == j * SSD_HEADS_PER_GROUP + exp_lane, 1.0, 0.0).astype(BF16)
        xdt_ref[...] = xc * _dot3(dt, e_hp)
        xd_ref[...] = xc * dexp_ref[...]


def ssd_sample_prep(xbc, conv_state_t, dt_raw, conv_w, conv_b, dt_bias, a_log, d_exp):
    w = SSD_GROUP_WIDTH
    s_blk = N_PROMPT // DEC_BATCH
    col = lambda j: (0, j)
    xcol = lambda j: (0, jnp.minimum(j, SSD_GROUPS - 1))
    const = lambda j: (0, 0)
    return pl.pallas_call(
        _ssd_sample_prep_body,
        grid=(SSD_CONV_DIM // w,),
        in_specs=[pl.BlockSpec((DEC_BATCH, w), lambda j: (s_blk, j)),
                  pl.BlockSpec((SSD_CONV - 1, DEC_BATCH, w), lambda j: (0, 0, j)),
                  pl.BlockSpec((SSD_CONV, w), col),
                  pl.BlockSpec((1, w), col),
                  pl.BlockSpec((DEC_BATCH, SSD_HEADS), lambda j: (s_blk, 0)),
                  pl.BlockSpec((1, SSD_HEADS), const),
                  pl.BlockSpec((1, SSD_HEADS), const),
                  pl.BlockSpec((1, w), xcol)],
        out_specs=[pl.BlockSpec((DEC_BATCH, w), col),
                   pl.BlockSpec((SSD_CONV - 1, DEC_BATCH, w), lambda j: (0, 0, j)),
                   pl.BlockSpec((DEC_BATCH, w), xcol),
                   pl.BlockSpec((DEC_BATCH, w), xcol),
                   pl.BlockSpec((DEC_BATCH, SSD_HEADS), const)],
        out_shape=[jax.ShapeDtypeStruct((DEC_BATCH, SSD_CONV_DIM), F32),
                   jax.ShapeDtypeStruct((SSD_CONV - 1, DEC_BATCH, SSD_CONV_DIM), F32),
                   jax.ShapeDtypeStruct((DEC_BATCH, SSD_D_INNER), F32),
                   jax.ShapeDtypeStruct((DEC_BATCH, SSD_D_INNER), F32),
                   jax.ShapeDtypeStruct((DEC_BATCH, SSD_HEADS), F32)],
        compiler_params=_cparams(("arbitrary",)),
        name="ssd_sample_prep",
    )(xbc, conv_state_t, conv_w, conv_b.reshape(1, -1), dt_raw, dt_bias.reshape(1, -1), a_log.reshape(1, -1), d_exp)


SSD_STEP_ROWS = 2 * SSD_GROUPS


def _ssd_state_step_body(s_ref, xdt_ref, xd_ref, b_ref, c_ref, dec_ref, *rest):
    s_out_ref, y_ref = rest[-2:]
    hp = SSD_D_INNER
    row = lax.broadcasted_iota(jnp.int32, (SSD_STEP_ROWS, hp), 0)
    grp = lax.broadcasted_iota(jnp.int32, (SSD_STEP_ROWS, hp), 1) // SSD_GROUP_WIDTH
    own = row == grp
    m1 = jnp.where(own, jnp.broadcast_to(xdt_ref[...], (SSD_STEP_ROWS, hp)), 0.0).astype(BF16)
    pad = jnp.zeros((SSD_STEP_ROWS - SSD_GROUPS, SSD_STATE), F32)
    b16 = jnp.concatenate([b_ref[...], pad], axis=0).astype(BF16)
    c16 = jnp.concatenate([c_ref[...], pad], axis=0).astype(BF16)
    u = _dot_tn(m1, b16)
    dg = jnp.broadcast_to(dec_ref[...], (SSD_HEADS, SSD_HEADS)).T
    p_row = lax.broadcasted_iota(jnp.int32, (SSD_HEADS * V7X_SUBLANES, SSD_HEADS), 0) // V7X_SUBLANES
    p_lane = lax.broadcasted_iota(jnp.int32, (SSD_HEADS * V7X_SUBLANES, SSD_HEADS), 1)
    p8 = jnp.where(p_row == p_lane, 1.0, 0.0).astype(BF16)
    dcol8 = _dot3_left(p8, dg)
    blocks = SSD_HEAD_DIM // V7X_SUBLANES
    shape4 = (SSD_HEADS, blocks, V7X_SUBLANES, SSD_STATE)
    s_new = (s_ref[...].reshape(shape4) * dcol8.reshape(SSD_HEADS, 1, V7X_SUBLANES, SSD_STATE)
             + u.reshape(shape4)).reshape(hp, SSD_STATE)
    s_out_ref[...] = s_new
    y_all = _dot_nt(c16, s_new.astype(BF16))
    y_ref[...] = jnp.sum(jnp.where(own, y_all, 0.0), axis=0, keepdims=True) + xd_ref[...]


def ssd_state_step(state, layer, xdt, xd, b_g, c_g, dec, out_alias=None):
    per_b = lambda b: (b, 0, 0)
    n_ssd = state.shape[0]
    st_blk = pl.BlockSpec((None, None, SSD_D_INNER, SSD_STATE), lambda b: (layer, b, 0, 0))
    in_specs = [st_blk,
                pl.BlockSpec((None, 1, SSD_D_INNER), per_b),
                pl.BlockSpec((None, 1, SSD_D_INNER), per_b),
                pl.BlockSpec((None, SSD_GROUPS, SSD_STATE), per_b),
                pl.BlockSpec((None, SSD_GROUPS, SSD_STATE), per_b),
                pl.BlockSpec((None, 1, SSD_HEADS), per_b)]
    args = [state, xdt, xd, b_g, c_g, dec]
    aliases = {}
    if out_alias is not None:
        in_specs.append(pl.BlockSpec(memory_space=pl.ANY))
        args.append(out_alias)
        aliases = {len(args) - 1: 0}
    return pl.pallas_call(
        _ssd_state_step_body,
        grid=(DEC_BATCH,),
        in_specs=in_specs,
        out_specs=[st_blk, pl.BlockSpec((None, 1, SSD_D_INNER), per_b)],
        out_shape=[jax.ShapeDtypeStruct((n_ssd, DEC_BATCH, SSD_D_INNER, SSD_STATE), F32),
                   jax.ShapeDtypeStruct((DEC_BATCH, 1, SSD_D_INNER), F32)],
        input_output_aliases=aliases,
        compiler_params=_cparams(("parallel",)),
        name="ssd_state_step",
    )(*args)


def _gated_norm_body(y_ref, z_ref, w_ref, mix_ref, o_ref):
    del mix_ref
    gw = SSD_GROUP_WIDTH
    for g in range(SSD_GROUPS):
        sl = slice(g * gw, (g + 1) * gw)
        gt = y_ref[:, sl] * _silu(z_ref[:, sl])
        gt = gt * lax.rsqrt(jnp.mean(gt * gt, axis=1, keepdims=True) + EPS) * w_ref[:, sl]
        o_ref[:, sl] = gt.astype(o_ref.dtype)


def gated_norm_sample(y_s, z, norm_w, mix):
    s_blk = N_PROMPT // DEC_BATCH
    return pl.pallas_call(
        _gated_norm_body,
        grid=(1,),
        in_specs=[pl.BlockSpec((DEC_BATCH, SSD_D_INNER), lambda i: (0, 0)),
                  pl.BlockSpec((DEC_BATCH, SSD_D_INNER), lambda i: (s_blk, 0)),
                  pl.BlockSpec((1, SSD_D_INNER), lambda i: (0, 0)),
                  pl.BlockSpec(memory_space=pl.ANY)],
        out_specs=pl.BlockSpec((DEC_BATCH, SSD_D_INNER), lambda i: (s_blk, 0)),
        out_shape=jax.ShapeDtypeStruct(mix.shape, mix.dtype),
        input_output_aliases={3: 0},
        compiler_params=_cparams(("arbitrary",)),
        name="gated_norm_sample",
    )(y_s, z, norm_w.reshape(1, -1), mix)


def _pack_weights(w_mem_kv, ssd_w_in, ssd_w_out, mla_w_in, mla_w_uq, mla_w_ukv, mla_w_out):
    def zeros(w, cols):
        return jnp.zeros(w.shape[:2] + (cols,), w.dtype)

    ssd_in = ssd_w_in.astype(BF16)
    ssd_x = ssd_w_in[:, :, SSD_MIX_COLS:].astype(BF16)
    o_pe = MLA_Q_LORA + MLA_KV_LORA
    o_g = o_pe + MLA_ROPE
    mla_in = jnp.concatenate([mla_w_in[:, :, :o_g], mla_w_in[:, :, o_pe:o_g],
                              zeros(mla_w_in, MLA_IN_OFF_GATE - MLA_A_COLS), mla_w_in[:, :, o_g:]], axis=2).astype(BF16)
    uq4 = mla_w_uq.reshape(mla_w_uq.shape[0], MLA_Q_LORA, MLA_HEADS, MLA_NOPE + MLA_ROPE)
    uq = jnp.concatenate([uq4[..., :MLA_NOPE].reshape(-1, MLA_Q_LORA, MLA_HEADS * MLA_NOPE),
                          uq4[..., MLA_NOPE:].reshape(-1, MLA_Q_LORA, MLA_HEADS * MLA_ROPE)], axis=2).astype(BF16)
    return dict(mem_kv=w_mem_kv.astype(BF16), ssd_in=ssd_in, ssd_x=ssd_x, ssd_out=ssd_w_out.astype(BF16), mla_in=mla_in,
                uq=uq, ukv=mla_w_ukv.astype(BF16), mla_out=mla_w_out.astype(BF16))


def _memory_branch(u, mem_n, w, w_in, layer, j, off_xq, off_xg, cache_mem_k, cache_mem_v):
    mk = matmul(mem_n, w["mem_kv"], layer, X_WIDTH, F32)
    mv = matmul(mem_n, w["mem_kv"], layer, X_WIDTH, F32, col0=X_WIDTH)
    xq = matmul(u, w_in, j, X_WIDTH, BF16, col0=off_xq)
    xg = matmul(u, w_in, j, X_WIDTH, F32, col0=off_xg)
    mem_p = cross_attend_prompt(xq, xg, mk, mv)
    mem_s = cross_attend_sample(xq[N_PROMPT:].astype(F32).reshape(DEC_BATCH, X_HEADS, X_HEAD_DIM),
                                xg[N_PROMPT:].reshape(DEC_BATCH, X_HEADS, X_HEAD_DIM),
                                cache_mem_k, cache_mem_v, layer)
    mem = jnp.concatenate([mem_p, mem_s.reshape(DEC_BATCH, X_WIDTH)], axis=0)
    return mem, mk, mv


def kernel(x_prompt, x_sample, cache_ckv, cache_kpe, state_ssm, state_conv, cache_mem_k, cache_mem_v, page_table, mem_prompt, norm_w, final_norm_w, mem_norm_w, w_mem_kv, ssd_w_in, ssd_conv_w, ssd_conv_b, ssd_dt_bias, ssd_A_log, ssd_D, ssd_norm_w, ssd_w_out, mla_w_in, mla_q_norm_w, mla_kv_norm_w, mla_w_uq, mla_w_ukv, mla_w_out):
    h = jnp.concatenate([x_prompt.reshape(N_PROMPT, D_MODEL), x_sample.reshape(DEC_BATCH, D_MODEL)], axis=0)
    mem2d = mem_prompt.reshape(BATCH * N_MEM, D_MODEL)
    state4 = state_ssm.reshape(state_ssm.shape[0], DEC_BATCH, SSD_D_INNER, SSD_STATE)
    cache_kpe_t = jnp.swapaxes(cache_kpe, 2, 3)
    cos2, sin2 = _rope_tables()
    w = _pack_weights(w_mem_kv, ssd_w_in, ssd_w_out, mla_w_in, mla_w_uq, mla_w_ukv, mla_w_out)

    ckv_p, kpe_p, ckv_s, kpe_s = [], [], [], []
    ssm_p, conv_p, conv_s, mk_out, mv_out = [], [], [], [], []
    ssm_s_all = None

    for i in range(DEPTH):
        j = i // 2
        mem_n, = rmsnorm(mem2d, mem_norm_w[i], [BF16])
        u, = rmsnorm(h, norm_w[i], [BF16])
        if i % 2 == 0:
            w_in = w["ssd_in"]
            z = matmul(u, w_in, j, SSD_D_INNER, F32, col0=SSD_IN_OFF_Z)
            xbc = matmul(u, w_in, j, SSD_CONV_DIM, F32, col0=SSD_IN_OFF_XBC)
            dt_raw = matmul(u, w_in, j, SSD_HEADS, F32, col0=SSD_IN_OFF_DT)
            mem, mk, mv = _memory_branch(u, mem_n, w, w["ssd_x"], i, j, 0, X_WIDTH, cache_mem_k, cache_mem_v)

            d_exp = jnp.repeat(ssd_D[j], SSD_HEAD_DIM).reshape(1, SSD_D_INNER)
            mix, ssm_pi = ssd_prompt(z, xbc, dt_raw, ssd_conv_w[j], ssd_conv_b[j], ssd_dt_bias[j], ssd_A_log[j],
                                     d_exp, ssd_norm_w[j])
            ssm_p.append(ssm_pi.reshape(BATCH, SSD_HEADS, SSD_HEAD_DIM, SSD_STATE))
            conv_p.append(jnp.stack([xbc[(b + 1) * SEQ - (SSD_CONV - 1):(b + 1) * SEQ] for b in range(BATCH)]))

            xconv, conv_new_t, xdt, xd, dec = ssd_sample_prep(
                xbc, jnp.transpose(state_conv[j], (1, 0, 2)), dt_raw, ssd_conv_w[j], ssd_conv_b[j],
                ssd_dt_bias[j], ssd_A_log[j], d_exp)
            conv_s.append(jnp.transpose(conv_new_t, (1, 0, 2)))
            gn = SSD_GROUPS * SSD_STATE
            b_g = xconv[:, SSD_D_INNER:SSD_D_INNER + gn].reshape(DEC_BATCH, SSD_GROUPS, SSD_STATE)
            c_g = xconv[:, SSD_D_INNER + gn:].reshape(DEC_BATCH, SSD_GROUPS, SSD_STATE)
            ssm_s_all, y_s = ssd_state_step(state4, j, xdt.reshape(DEC_BATCH, 1, SSD_D_INNER),
                                            xd.reshape(DEC_BATCH, 1, SSD_D_INNER), b_g, c_g,
                                            dec.reshape(DEC_BATCH, 1, SSD_HEADS), out_alias=ssm_s_all)
            mix = gated_norm_sample(y_s.reshape(DEC_BATCH, SSD_D_INNER), z, ssd_norm_w[j], mix)
            w_out = w["ssd_out"]
        else:
            w_in = w["mla_in"]
            a = matmul(u, w_in, j, MLA_A_COLS, F32)
            gate = matmul(u, w_in, j, MLA_WIDTH, F32, col0=MLA_IN_OFF_GATE)
            mem, mk, mv = _memory_branch(u, mem_n, w, w_in, i, j, MLA_IN_OFF_XQ, MLA_IN_OFF_XG, cache_mem_k, cache_mem_v)

            cqn, = rmsnorm(a, mla_q_norm_w[j], [BF16], width=MLA_Q_LORA, col_block=0)
            ckv, ckv_b, kpe, kpe_dup = mla_prep(a, mla_kv_norm_w[j], cos2, sin2)
            ckv_p.append(ckv[:N_PROMPT].reshape(BATCH, SEQ, MLA_KV_LORA))
            kpe_p.append(kpe[:N_PROMPT].reshape(BATCH, SEQ, MLA_ROPE))
            ckv_s.append(ckv[N_PROMPT:].reshape(DEC_BATCH, 1, MLA_KV_LORA))
            kpe_s.append(kpe[N_PROMPT:].reshape(DEC_BATCH, 1, MLA_ROPE))

            qn = matmul(cqn, w["uq"], j, MLA_HEADS * MLA_NOPE, BF16, out_scale=MLA_SCALE_LOG2)
            qp_raw = matmul(cqn, w["uq"], j, MLA_HEADS * MLA_ROPE, F32, col0=MLA_HEADS * MLA_NOPE)
            qp = rope_heads(qp_raw, cos2, sin2, BF16, MLA_SCALE_LOG2)
            kvup = matmul(ckv_b, w["ukv"], j, MLA_HEADS * (MLA_NOPE + MLA_V), BF16)
            mix = mla_flash_prompt(qn, qp, kvup, kpe_dup, gate)

            q_lat = mla_absorb_q(qn[N_PROMPT:], w["ukv"], j).reshape(DEC_BATCH, MLA_HEADS, MLA_KV_LORA)
            o_lat = mla_paged_sample(q_lat, qp[N_PROMPT:].reshape(DEC_BATCH, MLA_HEADS, MLA_ROPE),
                                     ckv_s[-1], kpe_s[-1], cache_ckv, cache_kpe_t, page_table, j)
            mix = mla_merge_sample(o_lat.reshape(DEC_BATCH, MLA_HEADS * MLA_KV_LORA), w["ukv"], j, gate, mix)
            w_out = w["mla_out"]

        mk_out.append(mk.reshape(BATCH, N_MEM, X_HEADS, X_HEAD_DIM))
        mv_out.append(mv.reshape(BATCH, N_MEM, X_HEADS, X_HEAD_DIM))
        h = matmul([mix, mem], w_out, j, D_MODEL, F32, res=h)

    y_prompt, = rmsnorm(h, final_norm_w, [F32], rows=N_PROMPT)
    y_sample, = rmsnorm(h, final_norm_w, [F32], row_start=N_PROMPT, rows=DEC_BATCH)
    ssm_sample = ssm_s_all.reshape(ssm_s_all.shape[0], DEC_BATCH, SSD_HEADS, SSD_HEAD_DIM, SSD_STATE)
    return (y_prompt.reshape(BATCH, SEQ, D_MODEL), y_sample.reshape(DEC_BATCH, 1, D_MODEL),
            jnp.stack(ckv_p), jnp.stack(kpe_p), jnp.stack(ckv_s), jnp.stack(kpe_s),
            jnp.stack(ssm_p), ssm_sample, jnp.stack(conv_p), jnp.stack(conv_s),
            jnp.stack(mk_out), jnp.stack(mv_out))
```

```python
import functools
import math

import jax
import jax.numpy as jnp
from jax import lax
from jax.experimental import pallas as pl
from jax.experimental.pallas import tpu as pltpu

F32 = jnp.float32
BF16 = jnp.bfloat16

D_MODEL = 4096
BATCH = 4
SEQ = 2048
DEPTH = 4
DEC_BATCH = 128
PAST_LEN = 8192
PAGE_SIZE = 128
N_PAGES = PAST_LEN // PAGE_SIZE
EPS = 1e-6

N_MEM = 256
X_HEADS = 4
X_HEAD_DIM = 768
X_WIDTH = X_HEADS * X_HEAD_DIM
X_SCALE = X_HEAD_DIM ** -0.5

SSD_D_INNER = 2 * D_MODEL
SSD_HEAD_DIM = 64
SSD_HEADS = SSD_D_INNER // SSD_HEAD_DIM
SSD_GROUPS = 8
SSD_HEADS_PER_GROUP = SSD_HEADS // SSD_GROUPS
SSD_GROUP_WIDTH = SSD_D_INNER // SSD_GROUPS
SSD_STATE = 128
SSD_CONV = 4
SSD_CONV_DIM = SSD_D_INNER + 2 * SSD_GROUPS * SSD_STATE
SSD_CHUNK = 128
SSD_MIX_COLS = SSD_D_INNER + SSD_CONV_DIM + SSD_HEADS

MLA_HEADS = 32
MLA_Q_LORA = 1024
MLA_KV_LORA = 512
MLA_NOPE = 128
MLA_ROPE = 64
MLA_V = 128
MLA_WIDTH = MLA_HEADS * MLA_V
MLA_MIX_COLS = MLA_Q_LORA + MLA_KV_LORA + MLA_ROPE + MLA_WIDTH
MLA_SCALE = (MLA_NOPE + MLA_ROPE) ** -0.5
MLA_SCALE_LOG2 = MLA_SCALE * math.log2(math.e)
ROPE_THETA = 10000.0

N_PROMPT = BATCH * SEQ
N_TOK = N_PROMPT + DEC_BATCH

V7X_LANES = 128
V7X_SUBLANES = 8
V7X_VMEM_LIMIT_BYTES = 56 * 1024 * 1024
V7X_VMEM_TILE_BUDGET_BYTES = 44 * 1024 * 1024

SSD_IN_OFF_Z = 0
SSD_IN_OFF_XBC = SSD_D_INNER
SSD_IN_OFF_DT = SSD_D_INNER + SSD_CONV_DIM
MLA_A_COLS = MLA_Q_LORA + MLA_KV_LORA + V7X_LANES
MLA_REST_OFF_GATE = 0
MLA_REST_OFF_XQ = MLA_WIDTH
MLA_REST_OFF_XG = MLA_WIDTH + X_WIDTH


def _cparams(semantics):
    return pltpu.CompilerParams(dimension_semantics=semantics, vmem_limit_bytes=V7X_VMEM_LIMIT_BYTES)


def _sigmoid(x):
    return 1.0 / (1.0 + jnp.exp(-x))


def _silu(x):
    return x * _sigmoid(x)


def _softplus(x):
    return jnp.maximum(x, 0.0) + jnp.log(1.0 + jnp.exp(-jnp.abs(x)))


def _dot(a, b):
    return jnp.dot(a, b, preferred_element_type=F32)


def _dot_nt(a, b):
    return lax.dot_general(a, b, (((1,), (1,)), ((), ())), preferred_element_type=F32)


def _dot_tn(a, b):
    return lax.dot_general(a, b, (((0,), (0,)), ((), ())), preferred_element_type=F32)


def _split3(x):
    hi = x.astype(BF16)
    r1 = x - hi.astype(F32)
    mid = r1.astype(BF16)
    lo = (r1 - mid.astype(F32)).astype(BF16)
    return hi, mid, lo


def _dot3(x, e):
    hi, mid, lo = _split3(x)
    return _dot(hi, e) + _dot(mid, e) + _dot(lo, e)


def _dot3_left(e, x):
    hi, mid, lo = _split3(x)
    return _dot(e, hi) + _dot(e, mid) + _dot(e, lo)


def _row_block(m, target):
    best = None
    for d in range(16, min(m, target) + 1, 16):
        if m % d == 0:
            best = d
    assert best is not None, (m, target)
    return best


def _rmsnorm_body(x_ref, w_ref, *o_refs):
    x = x_ref[...]
    y = x * lax.rsqrt(jnp.mean(x * x, axis=-1, keepdims=True) + EPS) * w_ref[...]
    for o_ref in o_refs:
        o_ref[...] = y.astype(o_ref.dtype)


def rmsnorm(x, w, out_dtypes, width=None, col_block=0, row_start=0, rows=None):
    m = rows or x.shape[0]
    d = width or x.shape[1]
    bm = _row_block(m, max(16, (2 * 1024 * 1024) // (d * 4)))
    assert row_start % bm == 0
    r0 = row_start // bm
    outs = pl.pallas_call(
        _rmsnorm_body,
        grid=(m // bm,),
        in_specs=[pl.BlockSpec((bm, d), lambda i: (i + r0, col_block)),
                  pl.BlockSpec((1, d), lambda i: (0, 0))],
        out_specs=[pl.BlockSpec((bm, d), lambda i: (i, 0)) for _ in out_dtypes],
        out_shape=[jax.ShapeDtypeStruct((m, d), dt) for dt in out_dtypes],
        compiler_params=_cparams(("parallel",)),
        name="rmsnorm",
    )(x, w.reshape(1, d))
    return outs


def _mm_body(*refs, nsrc, has_res, out_scale):
    x_refs = refs[:nsrc]
    w_ref = refs[nsrc]
    r_ref = refs[nsrc + 1] if has_res else None
    o_ref = refs[nsrc + 1 + int(has_res)]
    acc = None
    row = 0
    for x_ref in x_refs:
        k = x_ref.shape[1]
        part = _dot(x_ref[...], w_ref[row:row + k, :])
        acc = part if acc is None else acc + part
        row += k
    if out_scale is not None:
        acc = acc * out_scale
    if has_res:
        acc = acc + r_ref[...]
    o_ref[...] = acc.astype(o_ref.dtype)


def _mm_blocks(m, k, n, col0, out_bytes, has_res):
    bn_cands = [c for c in (1024, 768, 512, 384, 256, 128) if n % c == 0 and col0 % c == 0]
    if col0 == 0 and n <= 2048 and n not in bn_cands:
        bn_cands = [n] + bn_cands
    best = None
    for bn in bn_cands:
        for bm_target in (1040, 640, 520, 416, 320, 208, 128):
            bm = _row_block(m, bm_target)
            need = 2 * (bm * k * 2 + k * bn * 2 + bm * bn * out_bytes)
            if has_res:
                need += 2 * bm * bn * 4
            if need <= V7X_VMEM_TILE_BUDGET_BYTES:
                cand = (bm * bn, bm, bn)
                if best is None or cand > best:
                    best = cand
                break
    assert best is not None, (m, k, n)
    return best[1], best[2]


def matmul(xs, w, layer, n, out_dtype, col0=0, res=None, out_scale=None):
    if not isinstance(xs, (list, tuple)):
        xs = [xs]
    m = xs[0].shape[0]
    k = sum(x.shape[1] for x in xs)
    assert k == w.shape[1]
    has_res = res is not None
    bm, bn = _mm_blocks(m, k, n, col0, jnp.dtype(out_dtype).itemsize, has_res)
    assert col0 % bn == 0
    in_specs = [pl.BlockSpec((bm, x.shape[1]), lambda j, i: (i, 0)) for x in xs]
    in_specs.append(pl.BlockSpec((None, k, bn), lambda j, i: (layer, 0, col0 // bn + j)))
    args = list(xs) + [w]
    if has_res:
        in_specs.append(pl.BlockSpec((bm, bn), lambda j, i: (i, j)))
        args.append(res)
    return pl.pallas_call(
        functools.partial(_mm_body, nsrc=len(xs), has_res=has_res, out_scale=out_scale),
        grid=(n // bn, m // bm),
        in_specs=in_specs,
        out_specs=pl.BlockSpec((bm, bn), lambda j, i: (i, j)),
        out_shape=jax.ShapeDtypeStruct((m, n), out_dtype),
        compiler_params=_cparams(("parallel", "parallel")),
        name="matmul",
    )(*args)


def _rope_tables():
    half = MLA_ROPE // 2
    inv = ROPE_THETA ** (-jnp.arange(half, dtype=F32) / half)
    pos = jnp.concatenate([jnp.tile(jnp.arange(SEQ), BATCH),
                           jnp.full((DEC_BATCH,), PAST_LEN, jnp.int32)]).astype(F32)
    ang = pos[:, None] * inv[None, :]
    cos = jnp.cos(ang)
    sin = jnp.sin(ang)
    cos2 = jnp.tile(cos, (1, V7X_LANES // half))
    sin2 = jnp.tile(jnp.concatenate([-sin, sin], axis=1), (1, V7X_LANES // MLA_ROPE))
    return cos2, sin2


def _rotate_pairs(x, cos2, sin2, first_half):
    swapped = jnp.where(first_half, pltpu.roll(x, V7X_LANES - MLA_ROPE // 2, 1), pltpu.roll(x, MLA_ROPE // 2, 1))
    return x * cos2 + swapped * sin2


def _rope_body(x_ref, c_ref, s_ref, o_ref, *, out_scale):
    cos2 = c_ref[...]
    sin2 = s_ref[...]
    lane = lax.broadcasted_iota(jnp.int32, cos2.shape, 1)
    first_half = (lane % MLA_ROPE) < (MLA_ROPE // 2)
    for t in range(x_ref.shape[1] // V7X_LANES):
        sl = slice(t * V7X_LANES, (t + 1) * V7X_LANES)
        o_ref[:, sl] = (_rotate_pairs(x_ref[:, sl], cos2, sin2, first_half) * out_scale).astype(o_ref.dtype)


def rope_heads(x, cos2, sin2, out_dtype, out_scale):
    m, n = x.shape
    bm = _row_block(m, 320)
    return pl.pallas_call(
        functools.partial(_rope_body, out_scale=out_scale),
        grid=(m // bm,),
        in_specs=[pl.BlockSpec((bm, n), lambda i: (i, 0)),
                  pl.BlockSpec((bm, V7X_LANES), lambda i: (i, 0)),
                  pl.BlockSpec((bm, V7X_LANES), lambda i: (i, 0))],
        out_specs=pl.BlockSpec((bm, n), lambda i: (i, 0)),
        out_shape=jax.ShapeDtypeStruct((m, n), out_dtype),
        compiler_params=_cparams(("parallel",)),
        name="rope_heads",
    )(x, cos2, sin2)


def _mla_prep_body(ckv_in_ref, kpe_in_ref, w_ref, c_ref, s_ref, ckv_ref, ckvb_ref, kpe_ref, kpeb_ref):
    x = ckv_in_ref[...]
    y = x * lax.rsqrt(jnp.mean(x * x, axis=-1, keepdims=True) + EPS) * w_ref[...]
    ckv_ref[...] = y
    ckvb_ref[...] = y.astype(BF16)
    cos2 = c_ref[...]
    lane = lax.broadcasted_iota(jnp.int32, cos2.shape, 1)
    first_half = (lane % MLA_ROPE) < (MLA_ROPE // 2)
    r = _rotate_pairs(kpe_in_ref[...], cos2, s_ref[...], first_half)
    kpe_ref[...] = r[:, :MLA_ROPE]
    kpeb_ref[...] = r.astype(BF16)


def mla_prep(a, kv_norm_w, cos2, sin2):
    m = a.shape[0]
    bm = _row_block(m, 640)
    row = lambda i: (i, 0)
    return pl.pallas_call(
        _mla_prep_body,
        grid=(m // bm,),
        in_specs=[pl.BlockSpec((bm, MLA_KV_LORA), lambda i: (i, MLA_Q_LORA // MLA_KV_LORA)),
                  pl.BlockSpec((bm, V7X_LANES), lambda i: (i, (MLA_Q_LORA + MLA_KV_LORA) // V7X_LANES)),
                  pl.BlockSpec((1, MLA_KV_LORA), lambda i: (0, 0)),
                  pl.BlockSpec((bm, V7X_LANES), row),
                  pl.BlockSpec((bm, V7X_LANES), row)],
        out_specs=[pl.BlockSpec((bm, MLA_KV_LORA), row), pl.BlockSpec((bm, MLA_KV_LORA), row),
                   pl.BlockSpec((bm, MLA_ROPE), row), pl.BlockSpec((bm, V7X_LANES), row)],
        out_shape=[jax.ShapeDtypeStruct((m, MLA_KV_LORA), F32), jax.ShapeDtypeStruct((m, MLA_KV_LORA), BF16),
                   jax.ShapeDtypeStruct((m, MLA_ROPE), F32), jax.ShapeDtypeStruct((m, V7X_LANES), BF16)],
        compiler_params=_cparams(("parallel",)),
        name="mla_prep",
    )(a, a, kv_norm_w.reshape(1, MLA_KV_LORA), cos2, sin2)


FLASH_T = 512


def _flash_body(qn_ref, qp_ref, kv_ref, kpe_ref, g_ref, o_ref):
    qi = pl.program_id(2)
    t = FLASH_T
    lane = lax.broadcasted_iota(jnp.int32, (t, V7X_LANES), 1)
    qp = qp_ref[...]
    row = lax.broadcasted_iota(jnp.int32, (t, t), 0)
    col = lax.broadcasted_iota(jnp.int32, (t, t), 1)
    qs = []
    for h in range(2):
        own = (lane < MLA_ROPE) if h == 0 else (lane >= MLA_ROPE)
        qs.append(jnp.concatenate([qn_ref[:, h * MLA_NOPE:(h + 1) * MLA_NOPE],
                                   jnp.where(own, qp, jnp.zeros_like(qp))], axis=1))

    def step(j, carry, masked):
        off = pl.multiple_of(j * t, t)
        kpe = kpe_ref[pl.ds(off, t), :]
        out = []
        for h in range(2):
            m_prev, l_prev, acc = carry[h]
            kbase = h * (MLA_NOPE + MLA_V)
            k = jnp.concatenate([kv_ref[pl.ds(off, t), kbase:kbase + MLA_NOPE], kpe], axis=1)
            v = kv_ref[pl.ds(off, t), kbase + MLA_NOPE:kbase + MLA_NOPE + MLA_V]
            s = _dot_nt(qs[h], k)
            if masked:
                s = jnp.where(col <= row, s, -jnp.inf)
            m_new = jnp.maximum(m_prev, jnp.max(s, axis=1, keepdims=True))
            alpha = jnp.exp2(m_prev - m_new)
            p = jnp.exp2(s - m_new)
            l_new = l_prev * alpha + jnp.sum(p, axis=1, keepdims=True)
            out.append((m_new, l_new, acc * alpha + _dot(p.astype(BF16), v)))
        return tuple(out)

    init_h = (jnp.full((t, 1), -jnp.inf, F32), jnp.zeros((t, 1), F32), jnp.zeros((t, MLA_V), F32))
    carry = lax.fori_loop(0, qi, functools.partial(step, masked=False), (init_h, init_h))
    carry = step(qi, carry, True)
    for h in range(2):
        _, l_fin, acc = carry[h]
        sl = slice(h * MLA_V, (h + 1) * MLA_V)
        o_ref[:, sl] = (acc / l_fin * _silu(g_ref[:, sl])).astype(o_ref.dtype)


def mla_flash_prompt(qn, qp, kvup, kpe_dup, gate):
    nq = SEQ // FLASH_T
    row_blk = lambda b, hp, qi: (b * nq + qi, hp)
    return pl.pallas_call(
        _flash_body,
        grid=(BATCH, MLA_HEADS // 2, nq),
        in_specs=[pl.BlockSpec((FLASH_T, 2 * MLA_NOPE), row_blk),
                  pl.BlockSpec((FLASH_T, V7X_LANES), row_blk),
                  pl.BlockSpec((SEQ, 2 * (MLA_NOPE + MLA_V)), lambda b, hp, qi: (b, hp)),
                  pl.BlockSpec((SEQ, V7X_LANES), lambda b, hp, qi: (b, 0)),
                  pl.BlockSpec((FLASH_T, 2 * MLA_V), row_blk)],
        out_specs=pl.BlockSpec((FLASH_T, 2 * MLA_V), row_blk),
        out_shape=jax.ShapeDtypeStruct((N_TOK, MLA_WIDTH), BF16),
        compiler_params=_cparams(("parallel", "parallel", "arbitrary")),
        name="mla_flash_prompt",
    )(qn, qp, kvup, kpe_dup, gate)


PAGES_PER_STEP = 16
PAGE_SLOTS = 2


def _paged_body(pt_ref, ql_ref, qp_ref, cn_ref, kn_ref, ck_hbm, kp_hbm, o_ref,
                ck_buf, kp_buf, ck_sem, kp_sem, m_ref, l_ref, acc_ref, *, layer):
    pps = PAGES_PER_STEP
    step = pl.program_id(1)
    nsteps = pl.num_programs(1)
    t = pl.program_id(0) * nsteps + step
    total = pl.num_programs(0) * nsteps
    slot = lax.rem(t, PAGE_SLOTS)

    def page_copies(step_t, slot_t):
        copies = []
        for i in range(pps):
            page = pt_ref[step_t * pps + i]
            copies.append(pltpu.make_async_copy(ck_hbm.at[layer, page], ck_buf.at[slot_t, i], ck_sem.at[slot_t, i]))
            copies.append(pltpu.make_async_copy(kp_hbm.at[layer, page], kp_buf.at[slot_t, i], kp_sem.at[slot_t, i]))
        return copies

    @pl.when(t == 0)
    def _():
        for c in page_copies(t, slot):
            c.start()

    @pl.when(t + 1 < total)
    def _():
        for c in page_copies(t + 1, PAGE_SLOTS - 1 - slot):
            c.start()

    ql = ql_ref[...]
    qp = qp_ref[...]

    @pl.when(step == 0)
    def _():
        cn = cn_ref[...]
        m_ref[...] = (jnp.sum(ql.astype(F32) * cn, axis=1, keepdims=True)
                      + jnp.sum(qp.astype(F32) * kn_ref[...], axis=1, keepdims=True))
        l_ref[...] = jnp.ones_like(l_ref)
        acc_ref[...] = jnp.broadcast_to(cn, acc_ref.shape)

    for c in page_copies(t, slot):
        c.wait()

    kcs = [ck_buf[slot, i].astype(BF16) for i in range(pps)]
    s = jnp.concatenate([_dot_nt(ql, kcs[i]) + _dot(qp, kp_buf[slot, i].astype(BF16))
                         for i in range(pps)], axis=1)
    m_prev = m_ref[...]
    m_new = jnp.maximum(m_prev, jnp.max(s, axis=1, keepdims=True))
    alpha = jnp.exp2(m_prev - m_new)
    p = jnp.exp2(s - m_new)
    l_ref[...] = l_ref[...] * alpha + jnp.sum(p, axis=1, keepdims=True)
    pb = p.astype(BF16)
    acc = acc_ref[...] * alpha
    for i, kc in enumerate(kcs):
        acc = acc + _dot(pb[:, i * PAGE_SIZE:(i + 1) * PAGE_SIZE], kc)
    acc_ref[...] = acc
    m_ref[...] = m_new

    @pl.when(step == nsteps - 1)
    def _():
        o_ref[...] = acc_ref[...] / l_ref[...]


def mla_paged_sample(q_lat, q_pe, ckv_new, kpe_new, cache_ckv, cache_kpe_t, page_table, layer):
    pps = PAGES_PER_STEP
    nsteps = N_PAGES // pps
    per_b = lambda b, s, pt: (b, 0, 0)
    grid_spec = pltpu.PrefetchScalarGridSpec(
        num_scalar_prefetch=1,
        grid=(DEC_BATCH, nsteps),
        in_specs=[pl.BlockSpec((None, MLA_HEADS, MLA_KV_LORA), per_b),
                  pl.BlockSpec((None, MLA_HEADS, MLA_ROPE), per_b),
                  pl.BlockSpec((None, 1, MLA_KV_LORA), per_b),
                  pl.BlockSpec((None, 1, MLA_ROPE), per_b),
                  pl.BlockSpec(memory_space=pl.ANY),
                  pl.BlockSpec(memory_space=pl.ANY)],
        out_specs=pl.BlockSpec((None, MLA_HEADS, MLA_KV_LORA), per_b),
        scratch_shapes=[pltpu.VMEM((PAGE_SLOTS, pps, PAGE_SIZE, MLA_KV_LORA), F32),
                        pltpu.VMEM((PAGE_SLOTS, pps, MLA_ROPE, PAGE_SIZE), F32),
                        pltpu.SemaphoreType.DMA((PAGE_SLOTS, pps)),
                        pltpu.SemaphoreType.DMA((PAGE_SLOTS, pps)),
                        pltpu.VMEM((MLA_HEADS, 1), F32), pltpu.VMEM((MLA_HEADS, 1), F32),
                        pltpu.VMEM((MLA_HEADS, MLA_KV_LORA), F32)],
    )
    return pl.pallas_call(
        functools.partial(_paged_body, layer=layer),
        grid_spec=grid_spec,
        out_shape=jax.ShapeDtypeStruct((DEC_BATCH, MLA_HEADS, MLA_KV_LORA), F32),
        compiler_params=_cparams(("arbitrary", "arbitrary")),
        name="mla_paged_sample",
    )(page_table.reshape(-1), q_lat, q_pe, ckv_new, kpe_new, cache_ckv, cache_kpe_t)


def _absorb_body(x_ref, w_ref, o_ref):
    o_ref[...] = _dot_nt(x_ref[...], w_ref[...]).astype(o_ref.dtype)


def mla_absorb_q(q_nope, w_ukv, layer):
    m = q_nope.shape[0]
    return pl.pallas_call(
        _absorb_body,
        grid=(MLA_HEADS,),
        in_specs=[pl.BlockSpec((m, MLA_NOPE), lambda h: (0, h)),
                  pl.BlockSpec((None, MLA_KV_LORA, MLA_NOPE), lambda h: (layer, 0, 2 * h))],
        out_specs=pl.BlockSpec((m, MLA_KV_LORA), lambda h: (0, h)),
        out_shape=jax.ShapeDtypeStruct((m, MLA_HEADS * MLA_KV_LORA), BF16),
        compiler_params=_cparams(("parallel",)),
        name="mla_absorb_q",
    )(q_nope, w_ukv)


def _merge_body(x_ref, w_ref, g_ref, mix_ref, o_ref):
    del mix_ref
    o = _dot(x_ref[...].astype(BF16), w_ref[...])
    o_ref[...] = (o * _silu(g_ref[...])).astype(o_ref.dtype)


def mla_merge_sample(o_lat, w_ukv, layer, gate, mix):
    m = o_lat.shape[0]
    s_blk = N_PROMPT // m
    return pl.pallas_call(
        _merge_body,
        grid=(MLA_HEADS,),
        in_specs=[pl.BlockSpec((m, MLA_KV_LORA), lambda h: (0, h)),
                  pl.BlockSpec((None, MLA_KV_LORA, MLA_V), lambda h: (layer, 0, 2 * h + 1)),
                  pl.BlockSpec((m, MLA_V), lambda h: (s_blk, h)),
                  pl.BlockSpec(memory_space=pl.ANY)],
        out_specs=pl.BlockSpec((m, MLA_V), lambda h: (s_blk, h)),
        out_shape=jax.ShapeDtypeStruct(mix.shape, mix.dtype),
        input_output_aliases={3: 0},
        compiler_params=_cparams(("parallel",)),
        name="mla_merge_sample",
    )(o_lat, w_ukv, gate, mix)


CROSS_TQ = 256


def _cross_prompt_body(q_ref, g_ref, k_ref, v_ref, o_ref):
    for h in range(X_HEADS):
        sl = slice(h * X_HEAD_DIM, (h + 1) * X_HEAD_DIM)
        s = _dot_nt(q_ref[:, sl], k_ref[:, sl].astype(BF16)) * X_SCALE
        m = jnp.max(s, axis=1, keepdims=True)
        p = jnp.exp(s - m)
        p = p / jnp.sum(p, axis=1, keepdims=True)
        o = _dot(p.astype(BF16), v_ref[:, sl].astype(BF16))
        o_ref[:, sl] = (o * _silu(g_ref[:, sl])).astype(o_ref.dtype)


def cross_attend_prompt(xq, xg, mk, mv):
    nq = SEQ // CROSS_TQ
    row_blk = lambda b, qi: (b * nq + qi, 0)
    return pl.pallas_call(
        _cross_prompt_body,
        grid=(BATCH, nq),
        in_specs=[pl.BlockSpec((CROSS_TQ, X_WIDTH), row_blk),
                  pl.BlockSpec((CROSS_TQ, X_WIDTH), row_blk),
                  pl.BlockSpec((N_MEM, X_WIDTH), lambda b, qi: (b, 0)),
                  pl.BlockSpec((N_MEM, X_WIDTH), lambda b, qi: (b, 0))],
        out_specs=pl.BlockSpec((CROSS_TQ, X_WIDTH), row_blk),
        out_shape=jax.ShapeDtypeStruct((N_PROMPT, X_WIDTH), BF16),
        compiler_params=_cparams(("parallel", "parallel")),
        name="cross_attend_prompt",
    )(xq, xg, mk, mv)


def _cross_sample_body(q_ref, g_ref, k_ref, v_ref, o_ref):
    s = jnp.sum(k_ref[...] * q_ref[...][None], axis=-1, keepdims=True) * X_SCALE
    p = jnp.exp(s - jnp.max(s, axis=0, keepdims=True))
    o = jnp.sum(p * v_ref[...], axis=0) / jnp.sum(p, axis=0)
    o_ref[...] = (o * _silu(g_ref[...])).astype(o_ref.dtype)


def cross_attend_sample(xq_s, xg_s, cache_k, cache_v, layer):
    per_b = lambda b: (b, 0, 0)
    cache_blk = lambda b: (layer, b, 0, 0, 0)
    return pl.pallas_call(
        _cross_sample_body,
        grid=(DEC_BATCH,),
        in_specs=[pl.BlockSpec((None, X_HEADS, X_HEAD_DIM), per_b),
                  pl.BlockSpec((None, X_HEADS, X_HEAD_DIM), per_b),
                  pl.BlockSpec((None, None, N_MEM, X_HEADS, X_HEAD_DIM), cache_blk),
                  pl.BlockSpec((None, None, N_MEM, X_HEADS, X_HEAD_DIM), cache_blk)],
        out_specs=pl.BlockSpec((None, X_HEADS, X_HEAD_DIM), per_b),
        out_shape=jax.ShapeDtypeStruct((DEC_BATCH, X_HEADS, X_HEAD_DIM), BF16),
        compiler_params=_cparams(("parallel",)),
        name="cross_attend_sample",
    )(xq_s, xg_s, cache_k, cache_v)


SSD_CARRY_ROWS = 8


def _ssd_prompt_body(z_ref, xbc_ref, dt_ref, cw_ref, cb_ref, dtb_ref, alog_ref, dexp_ref, nw_ref,
                     o_ref, ssm_ref, xs_ref, ht_ref, at_ref):
    c = pl.program_id(1)
    q = SSD_CHUNK
    gw = SSD_GROUP_WIDTH
    hpg = SSD_HEADS_PER_GROUP
    base = SSD_CARRY_ROWS

    @pl.when(c == 0)
    def _():
        ht_ref[...] = jnp.zeros_like(ht_ref)
        xs_ref[0:base, :] = jnp.zeros((base, SSD_CONV_DIM), F32)

    xs_ref[base:base + q, :] = xbc_ref[...]

    def conv(off, width):
        cols = pl.ds(off, width)
        acc = cb_ref[:, cols]
        for tap in range(SSD_CONV):
            lo = base - (SSD_CONV - 1) + tap
            acc = acc + cw_ref[tap:tap + 1, cols] * xs_ref[lo:lo + q, cols]
        return _silu(acc)

    dt = _softplus(dt_ref[...] + dtb_ref[...])
    da = dt * (-jnp.exp(alog_ref[...]))
    ti = lax.broadcasted_iota(jnp.int32, (q, q), 0)
    tj = lax.broadcasted_iota(jnp.int32, (q, q), 1)
    causal = tj <= ti
    tri = jnp.where(causal, 1.0, 0.0).astype(BF16)
    a = _dot3_left(tri, da)
    at_ref[...] = a.T

    head_row = lax.broadcasted_iota(jnp.int32, (SSD_HEADS, gw), 0)
    exp_lane = lax.broadcasted_iota(jnp.int32, (SSD_HEADS, gw), 1) // SSD_HEAD_DIM
    col_row = lax.broadcasted_iota(jnp.int32, (SSD_HEADS, hpg * q), 0)
    col_lane = lax.broadcasted_iota(jnp.int32, (SSD_HEADS, hpg * q), 1) // q
    pair_lane = lax.broadcasted_iota(jnp.int32, (q, 2 * SSD_HEAD_DIM), 1)

    def group(g, _):
        ox = pl.multiple_of(g * gw, gw)
        ob = pl.multiple_of(SSD_D_INNER + g * SSD_STATE, SSD_STATE)
        oc = pl.multiple_of(SSD_D_INNER + SSD_GROUPS * SSD_STATE + g * SSD_STATE, SSD_STATE)
        xg = conv(ox, gw)
        bg = conv(ob, SSD_STATE)
        cg = conv(oc, SSD_STATE)
        bgb = bg.astype(BF16)
        cgb = cg.astype(BF16)

        e_hp = jnp.where(head_row == g * hpg + exp_lane, 1.0, 0.0).astype(BF16)
        dt_e = _dot3(dt, e_hp)
        a_e = _dot3(a, e_hp)
        a_last = a_e[q - 1:q, :]
        xdt = xg * dt_e
        xw = xdt * jnp.exp(a_last - a_e)
        xdtb = xdt.astype(BF16)

        cb = _dot_nt(cgb, bgb)
        e_col = jnp.where(col_row == g * hpg + col_lane, 1.0, 0.0).astype(BF16)
        a_col = _dot3(a, e_col)

        pieces = []
        for kp in range(hpg // 2):
            xpair = xdtb[:, kp * 2 * SSD_HEAD_DIM:(kp + 1) * 2 * SSD_HEAD_DIM]
            ypair = jnp.zeros((q, 2 * SSD_HEAD_DIM), F32)
            for hh in range(2):
                k = 2 * kp + hh
                a_row = at_ref[pl.ds(g * hpg + k, 1), :]
                seg = a_col[:, k * q:(k + 1) * q] - a_row
                decay = jnp.exp(jnp.where(causal, seg, -jnp.inf))
                own = (pair_lane < SSD_HEAD_DIM) if hh == 0 else (pair_lane >= SSD_HEAD_DIM)
                ypair = ypair + _dot((cb * decay).astype(BF16), jnp.where(own, xpair, jnp.zeros_like(xpair)))
            pieces.append(ypair)
        y_intra = jnp.concatenate(pieces, axis=1)

        ht_g = ht_ref[:, pl.ds(ox, gw)]
        y = y_intra + _dot(cgb, ht_g.astype(BF16)) * jnp.exp(a_e) + xg * dexp_ref[:, pl.ds(ox, gw)]
        zg = z_ref[:, pl.ds(ox, gw)]
        gt = y * _silu(zg)
        gt = gt * lax.rsqrt(jnp.mean(gt * gt, axis=1, keepdims=True) + EPS) * nw_ref[:, pl.ds(ox, gw)]
        o_ref[:, pl.ds(ox, gw)] = gt.astype(o_ref.dtype)

        ht_ref[:, pl.ds(ox, gw)] = ht_g * jnp.exp(a_last) + _dot(bg.T.astype(BF16), xw.astype(BF16))
        return 0

    lax.fori_loop(0, SSD_GROUPS, group, 0)
    xs_ref[0:base, :] = xs_ref[q:q + base, :]

    @pl.when(c == pl.num_programs(1) - 1)
    def _():
        for g in range(SSD_GROUPS):
            ssm_ref[g * gw:(g + 1) * gw, :] = ht_ref[:, g * gw:(g + 1) * gw].T


def ssd_prompt(z, xbc, dt_raw, conv_w, conv_b, dt_bias, a_log, d_exp, norm_w):
    nc = SEQ // SSD_CHUNK
    row_blk = lambda b, c: (b * nc + c, 0)
    const = lambda b, c: (0, 0)
    return pl.pallas_call(
        _ssd_prompt_body,
        grid=(BATCH, nc),
        in_specs=[pl.BlockSpec((SSD_CHUNK, SSD_D_INNER), row_blk),
                  pl.BlockSpec((SSD_CHUNK, SSD_CONV_DIM), row_blk),
                  pl.BlockSpec((SSD_CHUNK, SSD_HEADS), row_blk),
                  pl.BlockSpec((SSD_CONV, SSD_CONV_DIM), const),
                  pl.BlockSpec((1, SSD_CONV_DIM), const),
                  pl.BlockSpec((1, SSD_HEADS), const),
                  pl.BlockSpec((1, SSD_HEADS), const),
                  pl.BlockSpec((1, SSD_D_INNER), const),
                  pl.BlockSpec((1, SSD_D_INNER), const)],
        out_specs=[pl.BlockSpec((SSD_CHUNK, SSD_D_INNER), row_blk),
                   pl.BlockSpec((None, SSD_D_INNER, SSD_STATE), lambda b, c: (b, 0, 0))],
        out_shape=[jax.ShapeDtypeStruct((N_TOK, SSD_D_INNER), BF16),
                   jax.ShapeDtypeStruct((BATCH, SSD_D_INNER, SSD_STATE), F32)],
        scratch_shapes=[pltpu.VMEM((SSD_CARRY_ROWS + SSD_CHUNK, SSD_CONV_DIM), F32),
                        pltpu.VMEM((SSD_STATE, SSD_D_INNER), F32),
                        pltpu.VMEM((SSD_HEADS, SSD_CHUNK), F32)],
        compiler_params=_cparams(("parallel", "arbitrary")),
        name="ssd_prompt",
    )(z, xbc, dt_raw, conv_w, conv_b.reshape(1, -1), dt_bias.reshape(1, -1), a_log.reshape(1, -1),
      d_exp, norm_w.reshape(1, -1))


def _ssd_sample_prep_body(x_ref, st_ref, cw_ref, cb_ref, dtr_ref, dtb_ref, alog_ref, dexp_ref,
                          conv_ref, st_out_ref, xdt_ref, xd_ref, dec_ref):
    j = pl.program_id(0)
    gw = SSD_GROUP_WIDTH
    x = x_ref[...]
    acc = cb_ref[...] + cw_ref[SSD_CONV - 1:SSD_CONV, :] * x
    for tap in range(SSD_CONV - 1):
        acc = acc + cw_ref[tap:tap + 1, :] * st_ref[tap]
    xc = _silu(acc)
    conv_ref[...] = xc
    for tap in range(SSD_CONV - 2):
        st_out_ref[tap] = st_ref[tap + 1]
    st_out_ref[SSD_CONV - 2] = x

    dt = _softplus(dtr_ref[...] + dtb_ref[...])
    dec_ref[...] = jnp.exp(dt * (-jnp.exp(alog_ref[...])))

    @pl.when(j < SSD_GROUPS)
    def _():
        head_row = lax.broadcasted_iota(jnp.int32, (SSD_HEADS, gw), 0)
        exp_lane = lax.broadcasted_iota(jnp.int32, (SSD_HEADS, gw), 1) // SSD_HEAD_DIM
        e_hp = jnp.where(head_row == j * SSD_HEADS_PER_GROUP + exp_lane, 1.0, 0.0).astype(BF16)
        xdt_ref[...] = xc * _dot3(dt, e_hp)
        xd_ref[...] = xc * dexp_ref[...]


def ssd_sample_prep(xbc, conv_state_t, dt_raw, conv_w, conv_b, dt_bias, a_log, d_exp):
    w = SSD_GROUP_WIDTH
    s_blk = N_PROMPT // DEC_BATCH
    col = lambda j: (0, j)
    xcol = lambda j: (0, jnp.minimum(j, SSD_GROUPS - 1))
    const = lambda j: (0, 0)
    return pl.pallas_call(
        _ssd_sample_prep_body,
        grid=(SSD_CONV_DIM // w,),
        in_specs=[pl.BlockSpec((DEC_BATCH, w), lambda j: (s_blk, j)),
                  pl.BlockSpec((SSD_CONV - 1, DEC_BATCH, w), lambda j: (0, 0, j)),
                  pl.BlockSpec((SSD_CONV, w), col),
                  pl.BlockSpec((1, w), col),
                  pl.BlockSpec((DEC_BATCH, SSD_HEADS), lambda j: (s_blk, 0)),
                  pl.BlockSpec((1, SSD_HEADS), const),
                  pl.BlockSpec((1, SSD_HEADS), const),
                  pl.BlockSpec((1, w), xcol)],
        out_specs=[pl.BlockSpec((DEC_BATCH, w), col),
                   pl.BlockSpec((SSD_CONV - 1, DEC_BATCH, w), lambda j: (0, 0, j)),
                   pl.BlockSpec((DEC_BATCH, w), xcol),
                   pl.BlockSpec((DEC_BATCH, w), xcol),
                   pl.BlockSpec((DEC_BATCH, SSD_HEADS), const)],
        out_shape=[jax.ShapeDtypeStruct((DEC_BATCH, SSD_CONV_DIM), F32),
                   jax.ShapeDtypeStruct((SSD_CONV - 1, DEC_BATCH, SSD_CONV_DIM), F32),
                   jax.ShapeDtypeStruct((DEC_BATCH, SSD_D_INNER), F32),
                   jax.ShapeDtypeStruct((DEC_BATCH, SSD_D_INNER), F32),
                   jax.ShapeDtypeStruct((DEC_BATCH, SSD_HEADS), F32)],
        compiler_params=_cparams(("arbitrary",)),
        name="ssd_sample_prep",
    )(xbc, conv_state_t, conv_w, conv_b.reshape(1, -1), dt_raw, dt_bias.reshape(1, -1), a_log.reshape(1, -1), d_exp)


SSD_STEP_ROWS = 2 * SSD_GROUPS


def _ssd_state_step_body(s_ref, xdt_ref, xd_ref, b_ref, c_ref, dec_ref, *rest):
    s_out_ref, y_ref = rest[-2:]
    hp = SSD_D_INNER
    row = lax.broadcasted_iota(jnp.int32, (SSD_STEP_ROWS, hp), 0)
    grp = lax.broadcasted_iota(jnp.int32, (SSD_STEP_ROWS, hp), 1) // SSD_GROUP_WIDTH
    own = row == grp
    m1 = jnp.where(own, jnp.broadcast_to(xdt_ref[...], (SSD_STEP_ROWS, hp)), 0.0).astype(BF16)
    pad = jnp.zeros((SSD_STEP_ROWS - SSD_GROUPS, SSD_STATE), F32)
    b16 = jnp.concatenate([b_ref[...], pad], axis=0).astype(BF16)
    c16 = jnp.concatenate([c_ref[...], pad], axis=0).astype(BF16)
    u = _dot_tn(m1, b16)
    dg = jnp.broadcast_to(dec_ref[...], (SSD_HEADS, SSD_HEADS)).T
    p_row = lax.broadcasted_iota(jnp.int32, (SSD_HEADS * V7X_SUBLANES, SSD_HEADS), 0) // V7X_SUBLANES
    p_lane = lax.broadcasted_iota(jnp.int32, (SSD_HEADS * V7X_SUBLANES, SSD_HEADS), 1)
    p8 = jnp.where(p_row == p_lane, 1.0, 0.0).astype(BF16)
    dcol8 = _dot3_left(p8, dg)
    blocks = SSD_HEAD_DIM // V7X_SUBLANES
    shape4 = (SSD_HEADS, blocks, V7X_SUBLANES, SSD_STATE)
    s_new = (s_ref[...].reshape(shape4) * dcol8.reshape(SSD_HEADS, 1, V7X_SUBLANES, SSD_STATE)
             + u.reshape(shape4)).reshape(hp, SSD_STATE)
    s_out_ref[...] = s_new
    y_all = _dot_nt(c16, s_new.astype(BF16))
    y_ref[...] = jnp.sum(jnp.where(own, y_all, 0.0), axis=0, keepdims=True) + xd_ref[...]


def ssd_state_step(state, layer, xdt, xd, b_g, c_g, dec, out_alias=None):
    per_b = lambda b: (b, 0, 0)
    n_ssd = state.shape[0]
    st_blk = pl.BlockSpec((None, None, SSD_D_INNER, SSD_STATE), lambda b: (layer, b, 0, 0))
    in_specs = [st_blk,
                pl.BlockSpec((None, 1, SSD_D_INNER), per_b),
                pl.BlockSpec((None, 1, SSD_D_INNER), per_b),
                pl.BlockSpec((None, SSD_GROUPS, SSD_STATE), per_b),
                pl.BlockSpec((None, SSD_GROUPS, SSD_STATE), per_b),
                pl.BlockSpec((None, 1, SSD_HEADS), per_b)]
    args = [state, xdt, xd, b_g, c_g, dec]
    aliases = {}
    if out_alias is not None:
        in_specs.append(pl.BlockSpec(memory_space=pl.ANY))
        args.append(out_alias)
        aliases = {len(args) - 1: 0}
    return pl.pallas_call(
        _ssd_state_step_body,
        grid=(DEC_BATCH,),
        in_specs=in_specs,
        out_specs=[st_blk, pl.BlockSpec((None, 1, SSD_D_INNER), per_b)],
        out_shape=[jax.ShapeDtypeStruct((n_ssd, DEC_BATCH, SSD_D_INNER, SSD_STATE), F32),
                   jax.ShapeDtypeStruct((DEC_BATCH, 1, SSD_D_INNER), F32)],
        input_output_aliases=aliases,
        compiler_params=_cparams(("parallel",)),
        name="ssd_state_step",
    )(*args)


def _gated_norm_body(y_ref, z_ref, w_ref, mix_ref, o_ref):
    del mix_ref
    gw = SSD_GROUP_WIDTH
    for g in range(SSD_GROUPS):
        sl = slice(g * gw, (g + 1) * gw)
        gt = y_ref[:, sl] * _silu(z_ref[:, sl])
        gt = gt * lax.rsqrt(jnp.mean(gt * gt, axis=1, keepdims=True) + EPS) * w_ref[:, sl]
        o_ref[:, sl] = gt.astype(o_ref.dtype)


def gated_norm_sample(y_s, z, norm_w, mix):
    s_blk = N_PROMPT // DEC_BATCH
    return pl.pallas_call(
        _gated_norm_body,
        grid=(1,),
        in_specs=[pl.BlockSpec((DEC_BATCH, SSD_D_INNER), lambda i: (0, 0)),
                  pl.BlockSpec((DEC_BATCH, SSD_D_INNER), lambda i: (s_blk, 0)),
                  pl.BlockSpec((1, SSD_D_INNER), lambda i: (0, 0)),
                  pl.BlockSpec(memory_space=pl.ANY)],
        out_specs=pl.BlockSpec((DEC_BATCH, SSD_D_INNER), lambda i: (s_blk, 0)),
        out_shape=jax.ShapeDtypeStruct(mix.shape, mix.dtype),
        input_output_aliases={3: 0},
        compiler_params=_cparams(("arbitrary",)),
        name="gated_norm_sample",
    )(y_s, z, norm_w.reshape(1, -1), mix)


def _pack_weights(w_mem_kv, ssd_w_in, ssd_w_out, mla_w_in, mla_w_uq, mla_w_ukv, mla_w_out):
    ssd_in = ssd_w_in.astype(BF16)
    ssd_x = ssd_w_in[:, :, SSD_MIX_COLS:].astype(BF16)
    o_pe = MLA_Q_LORA + MLA_KV_LORA
    o_g = o_pe + MLA_ROPE
    mla_a = jnp.concatenate([mla_w_in[:, :, :o_g], mla_w_in[:, :, o_pe:o_g]], axis=2).astype(BF16)
    mla_rest = mla_w_in[:, :, o_g:].astype(BF16)
    uq4 = mla_w_uq.reshape(mla_w_uq.shape[0], MLA_Q_LORA, MLA_HEADS, MLA_NOPE + MLA_ROPE)
    uq = jnp.concatenate([uq4[..., :MLA_NOPE].reshape(-1, MLA_Q_LORA, MLA_HEADS * MLA_NOPE),
                          uq4[..., MLA_NOPE:].reshape(-1, MLA_Q_LORA, MLA_HEADS * MLA_ROPE)], axis=2).astype(BF16)
    return dict(mem_kv=w_mem_kv.astype(BF16), ssd_in=ssd_in, ssd_x=ssd_x, ssd_out=ssd_w_out.astype(BF16), mla_a=mla_a, mla_rest=mla_rest,
                uq=uq, ukv=mla_w_ukv.astype(BF16), mla_out=mla_w_out.astype(BF16))


def _memory_branch(u, mem_n, w, w_in, layer, j, off_xq, off_xg, cache_mem_k, cache_mem_v):
    mk = matmul(mem_n, w["mem_kv"], layer, X_WIDTH, F32)
    mv = matmul(mem_n, w["mem_kv"], layer, X_WIDTH, F32, col0=X_WIDTH)
    xq = matmul(u, w_in, j, X_WIDTH, BF16, col0=off_xq)
    xg = matmul(u, w_in, j, X_WIDTH, F32, col0=off_xg)
    mem_p = cross_attend_prompt(xq, xg, mk, mv)
    mem_s = cross_attend_sample(xq[N_PROMPT:].astype(F32).reshape(DEC_BATCH, X_HEADS, X_HEAD_DIM),
                                xg[N_PROMPT:].reshape(DEC_BATCH, X_HEADS, X_HEAD_DIM),
                                cache_mem_k, cache_mem_v, layer)
    mem = jnp.concatenate([mem_p, mem_s.reshape(DEC_BATCH, X_WIDTH)], axis=0)
    return mem, mk, mv


def kernel(x_prompt, x_sample, cache_ckv, cache_kpe, state_ssm, state_conv, cache_mem_k, cache_mem_v, page_table, mem_prompt, norm_w, final_norm_w, mem_norm_w, w_mem_kv, ssd_w_in, ssd_conv_w, ssd_conv_b, ssd_dt_bias, ssd_A_log, ssd_D, ssd_norm_w, ssd_w_out, mla_w_in, mla_q_norm_w, mla_kv_norm_w, mla_w_uq, mla_w_ukv, mla_w_out):
    h = jnp.concatenate([x_prompt.reshape(N_PROMPT, D_MODEL), x_sample.reshape(DEC_BATCH, D_MODEL)], axis=0)
    mem2d = mem_prompt.reshape(BATCH * N_MEM, D_MODEL)
    state4 = state_ssm.reshape(state_ssm.shape[0], DEC_BATCH, SSD_D_INNER, SSD_STATE)
    cache_kpe_t = jnp.swapaxes(cache_kpe, 2, 3)
    cos2, sin2 = _rope_tables()
    w = _pack_weights(w_mem_kv, ssd_w_in, ssd_w_out, mla_w_in, mla_w_uq, mla_w_ukv, mla_w_out)

    ckv_p, kpe_p, ckv_s, kpe_s = [], [], [], []
    ssm_p, conv_p, conv_s, mk_out, mv_out = [], [], [], [], []
    ssm_s_all = None

    for i in range(DEPTH):
        j = i // 2
        mem_n, = rmsnorm(mem2d, mem_norm_w[i], [BF16])
        u, = rmsnorm(h, norm_w[i], [BF16])
        if i % 2 == 0:
            w_in = w["ssd_in"]
            z = matmul(u, w_in, j, SSD_D_INNER, F32, col0=SSD_IN_OFF_Z)
            xbc = matmul(u, w_in, j, SSD_CONV_DIM, F32, col0=SSD_IN_OFF_XBC)
            dt_raw = matmul(u, w_in, j, SSD_HEADS, F32, col0=SSD_IN_OFF_DT)
            mem, mk, mv = _memory_branch(u, mem_n, w, w["ssd_x"], i, j, 0, X_WIDTH, cache_mem_k, cache_mem_v)

            d_exp = jnp.repeat(ssd_D[j], SSD_HEAD_DIM).reshape(1, SSD_D_INNER)
            mix, ssm_pi = ssd_prompt(z, xbc, dt_raw, ssd_conv_w[j], ssd_conv_b[j], ssd_dt_bias[j], ssd_A_log[j],
                                     d_exp, ssd_norm_w[j])
            ssm_p.append(ssm_pi.reshape(BATCH, SSD_HEADS, SSD_HEAD_DIM, SSD_STATE))
            conv_p.append(jnp.stack([xbc[(b + 1) * SEQ - (SSD_CONV - 1):(b + 1) * SEQ] for b in range(BATCH)]))

            xconv, conv_new_t, xdt, xd, dec = ssd_sample_prep(
                xbc, jnp.transpose(state_conv[j], (1, 0, 2)), dt_raw, ssd_conv_w[j], ssd_conv_b[j],
                ssd_dt_bias[j], ssd_A_log[j], d_exp)
            conv_s.append(jnp.transpose(conv_new_t, (1, 0, 2)))
            gn = SSD_GROUPS * SSD_STATE
            b_g = xconv[:, SSD_D_INNER:SSD_D_INNER + gn].reshape(DEC_BATCH, SSD_GROUPS, SSD_STATE)
            c_g = xconv[:, SSD_D_INNER + gn:].reshape(DEC_BATCH, SSD_GROUPS, SSD_STATE)
            ssm_s_all, y_s = ssd_state_step(state4, j, xdt.reshape(DEC_BATCH, 1, SSD_D_INNER),
                                            xd.reshape(DEC_BATCH, 1, SSD_D_INNER), b_g, c_g,
                                            dec.reshape(DEC_BATCH, 1, SSD_HEADS), out_alias=ssm_s_all)
            mix = gated_norm_sample(y_s.reshape(DEC_BATCH, SSD_D_INNER), z, ssd_norm_w[j], mix)
            w_out = w["ssd_out"]
        else:
            w_in = w["mla_rest"]
            a = matmul(u, w["mla_a"], j, MLA_A_COLS, F32)
            gate = matmul(u, w_in, j, MLA_WIDTH, F32, col0=MLA_REST_OFF_GATE)
            mem, mk, mv = _memory_branch(u, mem_n, w, w_in, i, j, MLA_REST_OFF_XQ, MLA_REST_OFF_XG, cache_mem_k, cache_mem_v)

            cqn, = rmsnorm(a, mla_q_norm_w[j], [BF16], width=MLA_Q_LORA, col_block=0)
            ckv, ckv_b, kpe, kpe_dup = mla_prep(a, mla_kv_norm_w[j], cos2, sin2)
            ckv_p.append(ckv[:N_PROMPT].reshape(BATCH, SEQ, MLA_KV_LORA))
            kpe_p.append(kpe[:N_PROMPT].reshape(BATCH, SEQ, MLA_ROPE))
            ckv_s.append(ckv[N_PROMPT:].reshape(DEC_BATCH, 1, MLA_KV_LORA))
            kpe_s.append(kpe[N_PROMPT:].reshape(DEC_BATCH, 1, MLA_ROPE))

            qn = matmul(cqn, w["uq"], j, MLA_HEADS * MLA_NOPE, BF16, out_scale=MLA_SCALE_LOG2)
            qp_raw = matmul(cqn, w["uq"], j, MLA_HEADS * MLA_ROPE, F32, col0=MLA_HEADS * MLA_NOPE)
            qp = rope_heads(qp_raw, cos2, sin2, BF16, MLA_SCALE_LOG2)
            kvup = matmul(ckv_b, w["ukv"], j, MLA_HEADS * (MLA_NOPE + MLA_V), BF16)
            mix = mla_flash_prompt(qn, qp, kvup, kpe_dup, gate)

            q_lat = mla_absorb_q(qn[N_PROMPT:], w["ukv"], j).reshape(DEC_BATCH, MLA_HEADS, MLA_KV_LORA)
            o_lat = mla_paged_sample(q_lat, qp[N_PROMPT:].reshape(DEC_BATCH, MLA_HEADS, MLA_ROPE),
                                     ckv_s[-1], kpe_s[-1], cache_ckv, cache_kpe_t, page_table, j)
            mix = mla_merge_sample(o_lat.reshape(DEC_BATCH, MLA_HEADS * MLA_KV_LORA), w["ukv"], j, gate, mix)
            w_out = w["mla_out"]

        mk_out.append(mk.reshape(BATCH, N_MEM, X_HEADS, X_HEAD_DIM))
        mv_out.append(mv.reshape(BATCH, N_MEM, X_HEADS, X_HEAD_DIM))
        h = matmul([mix, mem], w_out, j, D_MODEL, F32, res=h)

    y_prompt, = rmsnorm(h, final_norm_w, [F32], rows=N_PROMPT)
    y_sample, = rmsnorm(h, final_norm_w, [F32], row_start=N_PROMPT, rows=DEC_BATCH)
    ssm_sample = ssm_s_all.reshape(ssm_s_all.shape[0], DEC_BATCH, SSD_HEADS, SSD_HEAD_DIM, SSD_STATE)
    return (y_prompt.reshape(BATCH, SEQ, D_MODEL), y_sample.reshape(DEC_BATCH, 1, D_MODEL),
            jnp.stack(ckv_p), jnp.stack(kpe_p), jnp.stack(ckv_s), jnp.stack(kpe_s),
            jnp.stack(ssm_p), ssm_sample, jnp.stack(conv_p), jnp.stack(conv_s),
            jnp.stack(mk_out), jnp.stack(mv_out))
```

```python
import functools
import math

import jax
import jax.numpy as jnp
from jax import lax
from jax.experimental import pallas as pl
from jax.experimental.pallas import tpu as pltpu

F32 = jnp.float32
BF16 = jnp.bfloat16

D_MODEL = 4096
BATCH = 4
SEQ = 2048
DEPTH = 4
DEC_BATCH = 128
PAST_LEN = 8192
PAGE_SIZE = 128
N_PAGES = PAST_LEN // PAGE_SIZE
EPS = 1e-6

N_MEM = 256
X_HEADS = 4
X_HEAD_DIM = 768
X_WIDTH = X_HEADS * X_HEAD_DIM
X_SCALE = X_HEAD_DIM ** -0.5

SSD_D_INNER = 2 * D_MODEL
SSD_HEAD_DIM = 64
SSD_HEADS = SSD_D_INNER // SSD_HEAD_DIM
SSD_GROUPS = 8
SSD_HEADS_PER_GROUP = SSD_HEADS // SSD_GROUPS
SSD_GROUP_WIDTH = SSD_D_INNER // SSD_GROUPS
SSD_STATE = 128
SSD_CONV = 4
SSD_CONV_DIM = SSD_D_INNER + 2 * SSD_GROUPS * SSD_STATE
SSD_CHUNK = 128
SSD_MIX_COLS = SSD_D_INNER + SSD_CONV_DIM + SSD_HEADS

MLA_HEADS = 32
MLA_Q_LORA = 1024
MLA_KV_LORA = 512
MLA_NOPE = 128
MLA_ROPE = 64
MLA_V = 128
MLA_WIDTH = MLA_HEADS * MLA_V
MLA_MIX_COLS = MLA_Q_LORA + MLA_KV_LORA + MLA_ROPE + MLA_WIDTH
MLA_SCALE = (MLA_NOPE + MLA_ROPE) ** -0.5
MLA_SCALE_LOG2 = MLA_SCALE * math.log2(math.e)
ROPE_THETA = 10000.0

N_PROMPT = BATCH * SEQ
N_TOK = N_PROMPT + DEC_BATCH

V7X_LANES = 128
V7X_SUBLANES = 8
V7X_VMEM_LIMIT_BYTES = 56 * 1024 * 1024
V7X_VMEM_TILE_BUDGET_BYTES = 44 * 1024 * 1024

SSD_IN_OFF_Z = 0
SSD_IN_OFF_XBC = SSD_D_INNER
SSD_IN_OFF_DT = SSD_D_INNER + SSD_CONV_DIM
MLA_A_COLS = MLA_Q_LORA + MLA_KV_LORA + V7X_LANES
MLA_REST_OFF_GATE = 0
MLA_REST_OFF_XQ = MLA_WIDTH
MLA_REST_OFF_XG = MLA_WIDTH + X_WIDTH


def _cparams(semantics):
    return pltpu.CompilerParams(dimension_semantics=semantics, vmem_limit_bytes=V7X_VMEM_LIMIT_BYTES)


def _sigmoid(x):
    return 1.0 / (1.0 + jnp.exp(-x))


def _silu(x):
    return x * _sigmoid(x)


def _softplus(x):
    return jnp.maximum(x, 0.0) + jnp.log(1.0 + jnp.exp(-jnp.abs(x)))


def _dot(a, b):
    return jnp.dot(a, b, preferred_element_type=F32)


def _dot_nt(a, b):
    return lax.dot_general(a, b, (((1,), (1,)), ((), ())), preferred_element_type=F32)


def _dot_tn(a, b):
    return lax.dot_general(a, b, (((0,), (0,)), ((), ())), preferred_element_type=F32)


def _split3(x):
    hi = x.astype(BF16)
    r1 = x - hi.astype(F32)
    mid = r1.astype(BF16)
    lo = (r1 - mid.astype(F32)).astype(BF16)
    return hi, mid, lo


def _dot3(x, e):
    hi, mid, lo = _split3(x)
    return _dot(hi, e) + _dot(mid, e) + _dot(lo, e)


def _dot3_left(e, x):
    hi, mid, lo = _split3(x)
    return _dot(e, hi) + _dot(e, mid) + _dot(e, lo)


def _row_block(m, target):
    best = None
    for d in range(16, min(m, target) + 1, 16):
        if m % d == 0:
            best = d
    assert best is not None, (m, target)
    return best


def _rmsnorm_body(x_ref, w_ref, *o_refs):
    x = x_ref[...]
    y = x * lax.rsqrt(jnp.mean(x * x, axis=-1, keepdims=True) + EPS) * w_ref[...]
    for o_ref in o_refs:
        o_ref[...] = y.astype(o_ref.dtype)


def rmsnorm(x, w, out_dtypes, width=None, col_block=0, row_start=0, rows=None):
    m = rows or x.shape[0]
    d = width or x.shape[1]
    bm = _row_block(m, max(16, (2 * 1024 * 1024) // (d * 4)))
    assert row_start % bm == 0
    r0 = row_start // bm
    outs = pl.pallas_call(
        _rmsnorm_body,
        grid=(m // bm,),
        in_specs=[pl.BlockSpec((bm, d), lambda i: (i + r0, col_block)),
                  pl.BlockSpec((1, d), lambda i: (0, 0))],
        out_specs=[pl.BlockSpec((bm, d), lambda i: (i, 0)) for _ in out_dtypes],
        out_shape=[jax.ShapeDtypeStruct((m, d), dt) for dt in out_dtypes],
        compiler_params=_cparams(("parallel",)),
        name="rmsnorm",
    )(x, w.reshape(1, d))
    return outs


def _mm_body(*refs, nsrc, has_res, out_scale):
    x_refs = refs[:nsrc]
    w_ref = refs[nsrc]
    r_ref = refs[nsrc + 1] if has_res else None
    o_ref = refs[nsrc + 1 + int(has_res)]
    acc = None
    row = 0
    for x_ref in x_refs:
        k = x_ref.shape[1]
        part = _dot(x_ref[...], w_ref[row:row + k, :])
        acc = part if acc is None else acc + part
        row += k
    if out_scale is not None:
        acc = acc * out_scale
    if has_res:
        acc = acc + r_ref[...]
    o_ref[...] = acc.astype(o_ref.dtype)


def _mm_blocks(m, k, n, col0, out_bytes, has_res):
    bn_cands = [c for c in (1024, 768, 512, 384, 256, 128) if n % c == 0 and col0 % c == 0]
    if col0 == 0 and n <= 2048 and n not in bn_cands:
        bn_cands = [n] + bn_cands
    best = None
    for bn in bn_cands:
        for bm_target in (1040, 640, 520, 416, 320, 208, 128):
            bm = _row_block(m, bm_target)
            need = 2 * (bm * k * 2 + k * bn * 2 + bm * bn * out_bytes)
            if has_res:
                need += 2 * bm * bn * 4
            if need <= V7X_VMEM_TILE_BUDGET_BYTES:
                cand = (bm * bn, bm, bn)
                if best is None or cand > best:
                    best = cand
                break
    assert best is not None, (m, k, n)
    return best[1], best[2]


def matmul(xs, w, layer, n, out_dtype, col0=0, res=None, out_scale=None):
    if not isinstance(xs, (list, tuple)):
        xs = [xs]
    m = xs[0].shape[0]
    k = sum(x.shape[1] for x in xs)
    assert k == w.shape[1]
    has_res = res is not None
    bm, bn = _mm_blocks(m, k, n, col0, jnp.dtype(out_dtype).itemsize, has_res)
    assert col0 % bn == 0
    in_specs = [pl.BlockSpec((bm, x.shape[1]), lambda j, i: (i, 0)) for x in xs]
    in_specs.append(pl.BlockSpec((None, k, bn), lambda j, i: (layer, 0, col0 // bn + j)))
    args = list(xs) + [w]
    if has_res:
        in_specs.append(pl.BlockSpec((bm, bn), lambda j, i: (i, j)))
        args.append(res)
    return pl.pallas_call(
        functools.partial(_mm_body, nsrc=len(xs), has_res=has_res, out_scale=out_scale),
        grid=(n // bn, m // bm),
        in_specs=in_specs,
        out_specs=pl.BlockSpec((bm, bn), lambda j, i: (i, j)),
        out_shape=jax.ShapeDtypeStruct((m, n), out_dtype),
        compiler_params=_cparams(("parallel", "parallel")),
        name="matmul",
    )(*args)


def _rope_tables():
    half = MLA_ROPE // 2
    inv = ROPE_THETA ** (-jnp.arange(half, dtype=F32) / half)
    pos = jnp.concatenate([jnp.tile(jnp.arange(SEQ), BATCH),
                           jnp.full((DEC_BATCH,), PAST_LEN, jnp.int32)]).astype(F32)
    ang = pos[:, None] * inv[None, :]
    cos = jnp.cos(ang)
    sin = jnp.sin(ang)
    cos2 = jnp.tile(cos, (1, V7X_LANES // half))
    sin2 = jnp.tile(jnp.concatenate([-sin, sin], axis=1), (1, V7X_LANES // MLA_ROPE))
    return cos2, sin2


def _rotate_pairs(x, cos2, sin2, first_half):
    swapped = jnp.where(first_half, pltpu.roll(x, V7X_LANES - MLA_ROPE // 2, 1), pltpu.roll(x, MLA_ROPE // 2, 1))
    return x * cos2 + swapped * sin2


def _rope_body(x_ref, c_ref, s_ref, o_ref, *, out_scale):
    cos2 = c_ref[...]
    sin2 = s_ref[...]
    lane = lax.broadcasted_iota(jnp.int32, cos2.shape, 1)
    first_half = (lane % MLA_ROPE) < (MLA_ROPE // 2)
    for t in range(x_ref.shape[1] // V7X_LANES):
        sl = slice(t * V7X_LANES, (t + 1) * V7X_LANES)
        o_ref[:, sl] = (_rotate_pairs(x_ref[:, sl], cos2, sin2, first_half) * out_scale).astype(o_ref.dtype)


def rope_heads(x, cos2, sin2, out_dtype, out_scale):
    m, n = x.shape
    bm = _row_block(m, 320)
    return pl.pallas_call(
        functools.partial(_rope_body, out_scale=out_scale),
        grid=(m // bm,),
        in_specs=[pl.BlockSpec((bm, n), lambda i: (i, 0)),
                  pl.BlockSpec((bm, V7X_LANES), lambda i: (i, 0)),
                  pl.BlockSpec((bm, V7X_LANES), lambda i: (i, 0))],
        out_specs=pl.BlockSpec((bm, n), lambda i: (i, 0)),
        out_shape=jax.ShapeDtypeStruct((m, n), out_dtype),
        compiler_params=_cparams(("parallel",)),
        name="rope_heads",
    )(x, cos2, sin2)


def _mla_prep_body(ckv_in_ref, kpe_in_ref, w_ref, c_ref, s_ref, ckv_ref, ckvb_ref, kpe_ref, kpeb_ref):
    x = ckv_in_ref[...]
    y = x * lax.rsqrt(jnp.mean(x * x, axis=-1, keepdims=True) + EPS) * w_ref[...]
    ckv_ref[...] = y
    ckvb_ref[...] = y.astype(BF16)
    cos2 = c_ref[...]
    lane = lax.broadcasted_iota(jnp.int32, cos2.shape, 1)
    first_half = (lane % MLA_ROPE) < (MLA_ROPE // 2)
    r = _rotate_pairs(kpe_in_ref[...], cos2, s_ref[...], first_half)
    kpe_ref[...] = r[:, :MLA_ROPE]
    kpeb_ref[...] = r.astype(BF16)


def mla_prep(a, kv_norm_w, cos2, sin2):
    m = a.shape[0]
    bm = _row_block(m, 640)
    row = lambda i: (i, 0)
    return pl.pallas_call(
        _mla_prep_body,
        grid=(m // bm,),
        in_specs=[pl.BlockSpec((bm, MLA_KV_LORA), lambda i: (i, MLA_Q_LORA // MLA_KV_LORA)),
                  pl.BlockSpec((bm, V7X_LANES), lambda i: (i, (MLA_Q_LORA + MLA_KV_LORA) // V7X_LANES)),
                  pl.BlockSpec((1, MLA_KV_LORA), lambda i: (0, 0)),
                  pl.BlockSpec((bm, V7X_LANES), row),
                  pl.BlockSpec((bm, V7X_LANES), row)],
        out_specs=[pl.BlockSpec((bm, MLA_KV_LORA), row), pl.BlockSpec((bm, MLA_KV_LORA), row),
                   pl.BlockSpec((bm, MLA_ROPE), row), pl.BlockSpec((bm, V7X_LANES), row)],
        out_shape=[jax.ShapeDtypeStruct((m, MLA_KV_LORA), F32), jax.ShapeDtypeStruct((m, MLA_KV_LORA), BF16),
                   jax.ShapeDtypeStruct((m, MLA_ROPE), F32), jax.ShapeDtypeStruct((m, V7X_LANES), BF16)],
        compiler_params=_cparams(("parallel",)),
        name="mla_prep",
    )(a, a, kv_norm_w.reshape(1, MLA_KV_LORA), cos2, sin2)


FLASH_T = 512


def _flash_body(qn_ref, qp_ref, kv_ref, kpe_ref, g_ref, o_ref):
    qi = pl.program_id(2)
    t = FLASH_T
    lane = lax.broadcasted_iota(jnp.int32, (t, V7X_LANES), 1)
    qp = qp_ref[...]
    row = lax.broadcasted_iota(jnp.int32, (t, t), 0)
    col = lax.broadcasted_iota(jnp.int32, (t, t), 1)
    qs = []
    for h in range(2):
        own = (lane < MLA_ROPE) if h == 0 else (lane >= MLA_ROPE)
        qs.append(jnp.concatenate([qn_ref[:, h * MLA_NOPE:(h + 1) * MLA_NOPE],
                                   jnp.where(own, qp, jnp.zeros_like(qp))], axis=1))

    def step(j, carry, masked):
        off = pl.multiple_of(j * t, t)
        kpe = kpe_ref[pl.ds(off, t), :]
        out = []
        for h in range(2):
            m_prev, l_prev, acc = carry[h]
            kbase = h * (MLA_NOPE + MLA_V)
            k = jnp.concatenate([kv_ref[pl.ds(off, t), kbase:kbase + MLA_NOPE], kpe], axis=1)
            v = kv_ref[pl.ds(off, t), kbase + MLA_NOPE:kbase + MLA_NOPE + MLA_V]
            s = _dot_nt(qs[h], k)
            if masked:
                s = jnp.where(col <= row, s, -jnp.inf)
            m_new = jnp.maximum(m_prev, jnp.max(s, axis=1, keepdims=True))
            alpha = jnp.exp2(m_prev - m_new)
            p = jnp.exp2(s - m_new)
            l_new = l_prev * alpha + jnp.sum(p, axis=1, keepdims=True)
            out.append((m_new, l_new, acc * alpha + _dot(p.astype(BF16), v)))
        return tuple(out)

    init_h = (jnp.full((t, 1), -jnp.inf, F32), jnp.zeros((t, 1), F32), jnp.zeros((t, MLA_V), F32))
    carry = lax.fori_loop(0, qi, functools.partial(step, masked=False), (init_h, init_h))
    carry = step(qi, carry, True)
    for h in range(2):
        _, l_fin, acc = carry[h]
        sl = slice(h * MLA_V, (h + 1) * MLA_V)
        o_ref[:, sl] = (acc / l_fin * _silu(g_ref[:, sl])).astype(o_ref.dtype)


def mla_flash_prompt(qn, qp, kvup, kpe_dup, gate):
    nq = SEQ // FLASH_T
    row_blk = lambda b, hp, qi: (b * nq + qi, hp)
    return pl.pallas_call(
        _flash_body,
        grid=(BATCH, MLA_HEADS // 2, nq),
        in_specs=[pl.BlockSpec((FLASH_T, 2 * MLA_NOPE), row_blk),
                  pl.BlockSpec((FLASH_T, V7X_LANES), row_blk),
                  pl.BlockSpec((SEQ, 2 * (MLA_NOPE + MLA_V)), lambda b, hp, qi: (b, hp)),
                  pl.BlockSpec((SEQ, V7X_LANES), lambda b, hp, qi: (b, 0)),
                  pl.BlockSpec((FLASH_T, 2 * MLA_V), row_blk)],
        out_specs=pl.BlockSpec((FLASH_T, 2 * MLA_V), row_blk),
        out_shape=jax.ShapeDtypeStruct((N_TOK, MLA_WIDTH), BF16),
        compiler_params=_cparams(("parallel", "parallel", "arbitrary")),
        name="mla_flash_prompt",
    )(qn, qp, kvup, kpe_dup, gate)


PAGES_PER_STEP = 16
PAGE_SLOTS = 2


def _paged_body(pt_ref, ql_ref, qp_ref, cn_ref, kn_ref, ck_hbm, kp_hbm, o_ref,
                ck_buf, kp_buf, ck_sem, kp_sem, m_ref, l_ref, acc_ref, *, layer):
    pps = PAGES_PER_STEP
    step = pl.program_id(1)
    nsteps = pl.num_programs(1)
    t = pl.program_id(0) * nsteps + step
    total = pl.num_programs(0) * nsteps
    slot = lax.rem(t, PAGE_SLOTS)

    def page_copies(step_t, slot_t):
        copies = []
        for i in range(pps):
            page = pt_ref[step_t * pps + i]
            copies.append(pltpu.make_async_copy(ck_hbm.at[layer, page], ck_buf.at[slot_t, i], ck_sem.at[slot_t, i]))
            copies.append(pltpu.make_async_copy(kp_hbm.at[layer, page], kp_buf.at[slot_t, i], kp_sem.at[slot_t, i]))
        return copies

    @pl.when(t == 0)
    def _():
        for c in page_copies(t, slot):
            c.start()

    @pl.when(t + 1 < total)
    def _():
        for c in page_copies(t + 1, PAGE_SLOTS - 1 - slot):
            c.start()

    ql = ql_ref[...]
    qp = qp_ref[...]

    @pl.when(step == 0)
    def _():
        cn = cn_ref[...]
        m_ref[...] = (jnp.sum(ql.astype(F32) * cn, axis=1, keepdims=True)
                      + jnp.sum(qp.astype(F32) * kn_ref[...], axis=1, keepdims=True))
        l_ref[...] = jnp.ones_like(l_ref)
        acc_ref[...] = jnp.broadcast_to(cn, acc_ref.shape)

    for c in page_copies(t, slot):
        c.wait()

    kcs = [ck_buf[slot, i].astype(BF16) for i in range(pps)]
    s = jnp.concatenate([_dot_nt(ql, kcs[i]) + _dot(qp, kp_buf[slot, i].astype(BF16))
                         for i in range(pps)], axis=1)
    m_prev = m_ref[...]
    m_new = jnp.maximum(m_prev, jnp.max(s, axis=1, keepdims=True))
    alpha = jnp.exp2(m_prev - m_new)
    p = jnp.exp2(s - m_new)
    l_ref[...] = l_ref[...] * alpha + jnp.sum(p, axis=1, keepdims=True)
    pb = p.astype(BF16)
    acc = acc_ref[...] * alpha
    for i, kc in enumerate(kcs):
        acc = acc + _dot(pb[:, i * PAGE_SIZE:(i + 1) * PAGE_SIZE], kc)
    acc_ref[...] = acc
    m_ref[...] = m_new

    @pl.when(step == nsteps - 1)
    def _():
        o_ref[...] = acc_ref[...] / l_ref[...]


def mla_paged_sample(q_lat, q_pe, ckv_new, kpe_new, cache_ckv, cache_kpe_t, page_table, layer):
    pps = PAGES_PER_STEP
    nsteps = N_PAGES // pps
    per_b = lambda b, s, pt: (b, 0, 0)
    grid_spec = pltpu.PrefetchScalarGridSpec(
        num_scalar_prefetch=1,
        grid=(DEC_BATCH, nsteps),
        in_specs=[pl.BlockSpec((None, MLA_HEADS, MLA_KV_LORA), per_b),
                  pl.BlockSpec((None, MLA_HEADS, MLA_ROPE), per_b),
                  pl.BlockSpec((None, 1, MLA_KV_LORA), per_b),
                  pl.BlockSpec((None, 1, MLA_ROPE), per_b),
                  pl.BlockSpec(memory_space=pl.ANY),
                  pl.BlockSpec(memory_space=pl.ANY)],
        out_specs=pl.BlockSpec((None, MLA_HEADS, MLA_KV_LORA), per_b),
        scratch_shapes=[pltpu.VMEM((PAGE_SLOTS, pps, PAGE_SIZE, MLA_KV_LORA), F32),
                        pltpu.VMEM((PAGE_SLOTS, pps, MLA_ROPE, PAGE_SIZE), F32),
                        pltpu.SemaphoreType.DMA((PAGE_SLOTS, pps)),
                        pltpu.SemaphoreType.DMA((PAGE_SLOTS, pps)),
                        pltpu.VMEM((MLA_HEADS, 1), F32), pltpu.VMEM((MLA_HEADS, 1), F32),
                        pltpu.VMEM((MLA_HEADS, MLA_KV_LORA), F32)],
    )
    return pl.pallas_call(
        functools.partial(_paged_body, layer=layer),
        grid_spec=grid_spec,
        out_shape=jax.ShapeDtypeStruct((DEC_BATCH, MLA_HEADS, MLA_KV_LORA), F32),
        compiler_params=_cparams(("arbitrary", "arbitrary")),
        name="mla_paged_sample",
    )(page_table.reshape(-1), q_lat, q_pe, ckv_new, kpe_new, cache_ckv, cache_kpe_t)


def _absorb_body(x_ref, w_ref, o_ref):
    o_ref[...] = _dot_nt(x_ref[...], w_ref[...]).astype(o_ref.dtype)


def mla_absorb_q(q_nope, w_ukv, layer):
    m = q_nope.shape[0]
    return pl.pallas_call(
        _absorb_body,
        grid=(MLA_HEADS,),
        in_specs=[pl.BlockSpec((m, MLA_NOPE), lambda h: (0, h)),
                  pl.BlockSpec((None, MLA_KV_LORA, MLA_NOPE), lambda h: (layer, 0, 2 * h))],
        out_specs=pl.BlockSpec((m, MLA_KV_LORA), lambda h: (0, h)),
        out_shape=jax.ShapeDtypeStruct((m, MLA_HEADS * MLA_KV_LORA), BF16),
        compiler_params=_cparams(("parallel",)),
        name="mla_absorb_q",
    )(q_nope, w_ukv)


def _merge_body(x_ref, w_ref, g_ref, mix_ref, o_ref):
    del mix_ref
    o = _dot(x_ref[...].astype(BF16), w_ref[...])
    o_ref[...] = (o * _silu(g_ref[...])).astype(o_ref.dtype)


def mla_merge_sample(o_lat, w_ukv, layer, gate, mix):
    m = o_lat.shape[0]
    s_blk = N_PROMPT // m
    return pl.pallas_call(
        _merge_body,
        grid=(MLA_HEADS,),
        in_specs=[pl.BlockSpec((m, MLA_KV_LORA), lambda h: (0, h)),
                  pl.BlockSpec((None, MLA_KV_LORA, MLA_V), lambda h: (layer, 0, 2 * h + 1)),
                  pl.BlockSpec((m, MLA_V), lambda h: (s_blk, h)),
                  pl.BlockSpec(memory_space=pl.ANY)],
        out_specs=pl.BlockSpec((m, MLA_V), lambda h: (s_blk, h)),
        out_shape=jax.ShapeDtypeStruct(mix.shape, mix.dtype),
        input_output_aliases={3: 0},
        compiler_params=_cparams(("parallel",)),
        name="mla_merge_sample",
    )(o_lat, w_ukv, gate, mix)


CROSS_TQ = 256


def _cross_prompt_body(q_ref, g_ref, k_ref, v_ref, o_ref):
    for h in range(X_HEADS):
        sl = slice(h * X_HEAD_DIM, (h + 1) * X_HEAD_DIM)
        s = _dot_nt(q_ref[:, sl], k_ref[:, sl].astype(BF16)) * X_SCALE
        m = jnp.max(s, axis=1, keepdims=True)
        p = jnp.exp(s - m)
        p = p / jnp.sum(p, axis=1, keepdims=True)
        o = _dot(p.astype(BF16), v_ref[:, sl].astype(BF16))
        o_ref[:, sl] = (o * _silu(g_ref[:, sl])).astype(o_ref.dtype)


def cross_attend_prompt(xq, xg, mk, mv):
    nq = SEQ // CROSS_TQ
    row_blk = lambda b, qi: (b * nq + qi, 0)
    return pl.pallas_call(
        _cross_prompt_body,
        grid=(BATCH, nq),
        in_specs=[pl.BlockSpec((CROSS_TQ, X_WIDTH), row_blk),
                  pl.BlockSpec((CROSS_TQ, X_WIDTH), row_blk),
                  pl.BlockSpec((N_MEM, X_WIDTH), lambda b, qi: (b, 0)),
                  pl.BlockSpec((N_MEM, X_WIDTH), lambda b, qi: (b, 0))],
        out_specs=pl.BlockSpec((CROSS_TQ, X_WIDTH), row_blk),
        out_shape=jax.ShapeDtypeStruct((N_TOK, X_WIDTH), BF16),
        compiler_params=_cparams(("parallel", "parallel")),
        name="cross_attend_prompt",
    )(xq, xg, mk, mv)


def _fill_rows_body(src_ref, dst_ref, o_ref):
    del dst_ref
    o_ref[...] = src_ref[...]


def fill_sample_rows(dst, src):
    w = dst.shape[1]
    return pl.pallas_call(
        _fill_rows_body,
        grid=(1,),
        in_specs=[pl.BlockSpec((DEC_BATCH, w), lambda i: (0, 0)),
                  pl.BlockSpec(memory_space=pl.ANY)],
        out_specs=pl.BlockSpec((DEC_BATCH, w), lambda i: (N_PROMPT // DEC_BATCH, 0)),
        out_shape=jax.ShapeDtypeStruct(dst.shape, dst.dtype),
        input_output_aliases={1: 0},
        compiler_params=_cparams(("arbitrary",)),
        name="fill_sample_rows",
    )(src, dst)


def _cross_sample_body(q_ref, g_ref, k_ref, v_ref, o_ref):
    s = jnp.sum(k_ref[...] * q_ref[...][None], axis=-1, keepdims=True) * X_SCALE
    p = jnp.exp(s - jnp.max(s, axis=0, keepdims=True))
    o = jnp.sum(p * v_ref[...], axis=0) / jnp.sum(p, axis=0)
    o_ref[...] = (o * _silu(g_ref[...])).astype(o_ref.dtype)


def cross_attend_sample(xq_s, xg_s, cache_k, cache_v, layer):
    per_b = lambda b: (b, 0, 0)
    cache_blk = lambda b: (layer, b, 0, 0, 0)
    return pl.pallas_call(
        _cross_sample_body,
        grid=(DEC_BATCH,),
        in_specs=[pl.BlockSpec((None, X_HEADS, X_HEAD_DIM), per_b),
                  pl.BlockSpec((None, X_HEADS, X_HEAD_DIM), per_b),
                  pl.BlockSpec((None, None, N_MEM, X_HEADS, X_HEAD_DIM), cache_blk),
                  pl.BlockSpec((None, None, N_MEM, X_HEADS, X_HEAD_DIM), cache_blk)],
        out_specs=pl.BlockSpec((None, X_HEADS, X_HEAD_DIM), per_b),
        out_shape=jax.ShapeDtypeStruct((DEC_BATCH, X_HEADS, X_HEAD_DIM), BF16),
        compiler_params=_cparams(("parallel",)),
        name="cross_attend_sample",
    )(xq_s, xg_s, cache_k, cache_v)


SSD_CARRY_ROWS = 8


def _ssd_prompt_body(z_ref, xbc_ref, dt_ref, cw_ref, cb_ref, dtb_ref, alog_ref, dexp_ref, nw_ref,
                     o_ref, ssm_ref, xs_ref, ht_ref, at_ref, ehp_ref, ecol_ref):
    c = pl.program_id(1)
    q = SSD_CHUNK
    gw = SSD_GROUP_WIDTH
    hpg = SSD_HEADS_PER_GROUP
    base = SSD_CARRY_ROWS

    @pl.when(c == 0)
    def _():
        ht_ref[...] = jnp.zeros_like(ht_ref)
        xs_ref[0:base, :] = jnp.zeros((base, SSD_CONV_DIM), F32)
        for g in range(SSD_GROUPS):
            row = lax.broadcasted_iota(jnp.int32, (SSD_HEADS, gw), 0)
            lane = lax.broadcasted_iota(jnp.int32, (SSD_HEADS, gw), 1) // SSD_HEAD_DIM
            ehp_ref[:, g * gw:(g + 1) * gw] = jnp.where(row == g * hpg + lane, 1.0, 0.0).astype(BF16)
            row = lax.broadcasted_iota(jnp.int32, (SSD_HEADS, hpg * q), 0)
            lane = lax.broadcasted_iota(jnp.int32, (SSD_HEADS, hpg * q), 1) // q
            ecol_ref[:, g * hpg * q:(g + 1) * hpg * q] = jnp.where(row == g * hpg + lane, 1.0, 0.0).astype(BF16)

    xs_ref[base:base + q, :] = xbc_ref[...]

    def conv(off, width):
        cols = pl.ds(off, width)
        acc = cb_ref[:, cols]
        for tap in range(SSD_CONV):
            lo = base - (SSD_CONV - 1) + tap
            acc = acc + cw_ref[tap:tap + 1, cols] * xs_ref[lo:lo + q, cols]
        return _silu(acc)

    dt = _softplus(dt_ref[...] + dtb_ref[...])
    da = dt * (-jnp.exp(alog_ref[...]))
    ti = lax.broadcasted_iota(jnp.int32, (q, q), 0)
    tj = lax.broadcasted_iota(jnp.int32, (q, q), 1)
    causal = tj <= ti
    tri = jnp.where(causal, 1.0, 0.0).astype(BF16)
    a = _dot3_left(tri, da)
    at_ref[...] = a.T

    pair_lane = lax.broadcasted_iota(jnp.int32, (q, 2 * SSD_HEAD_DIM), 1)
    dt3 = _split3(dt)
    a3 = _split3(a)

    def expand3(parts, e):
        return _dot(parts[0], e) + _dot(parts[1], e) + _dot(parts[2], e)

    def group(g, _):
        ox = pl.multiple_of(g * gw, gw)
        ob = pl.multiple_of(SSD_D_INNER + g * SSD_STATE, SSD_STATE)
        oc = pl.multiple_of(SSD_D_INNER + SSD_GROUPS * SSD_STATE + g * SSD_STATE, SSD_STATE)
        xg = conv(ox, gw)
        bg = conv(ob, SSD_STATE)
        cg = conv(oc, SSD_STATE)
        bgb = bg.astype(BF16)
        cgb = cg.astype(BF16)

        e_hp = ehp_ref[:, pl.ds(ox, gw)]
        dt_e = expand3(dt3, e_hp)
        a_e = expand3(a3, e_hp)
        a_last = a_e[q - 1:q, :]
        xdt = xg * dt_e
        xw = xdt * jnp.exp(a_last - a_e)
        xdtb = xdt.astype(BF16)

        cb = _dot_nt(cgb, bgb)
        e_col = ecol_ref[:, pl.ds(pl.multiple_of(g * hpg * q, hpg * q), hpg * q)]
        a_col = expand3(a3, e_col)

        pieces = []
        for kp in range(hpg // 2):
            xpair = xdtb[:, kp * 2 * SSD_HEAD_DIM:(kp + 1) * 2 * SSD_HEAD_DIM]
            ypair = jnp.zeros((q, 2 * SSD_HEAD_DIM), F32)
            for hh in range(2):
                k = 2 * kp + hh
                a_row = at_ref[pl.ds(g * hpg + k, 1), :]
                seg = a_col[:, k * q:(k + 1) * q] - a_row
                decay = jnp.exp(jnp.where(causal, seg, -jnp.inf))
                own = (pair_lane < SSD_HEAD_DIM) if hh == 0 else (pair_lane >= SSD_HEAD_DIM)
                ypair = ypair + _dot((cb * decay).astype(BF16), jnp.where(own, xpair, jnp.zeros_like(xpair)))
            pieces.append(ypair)
        y_intra = jnp.concatenate(pieces, axis=1)

        ht_g = ht_ref[:, pl.ds(ox, gw)]
        y = y_intra + _dot(cgb, ht_g.astype(BF16)) * jnp.exp(a_e) + xg * dexp_ref[:, pl.ds(ox, gw)]
        zg = z_ref[:, pl.ds(ox, gw)]
        gt = y * _silu(zg)
        gt = gt * lax.rsqrt(jnp.mean(gt * gt, axis=1, keepdims=True) + EPS) * nw_ref[:, pl.ds(ox, gw)]
        o_ref[:, pl.ds(ox, gw)] = gt.astype(o_ref.dtype)

        ht_ref[:, pl.ds(ox, gw)] = ht_g * jnp.exp(a_last) + _dot(bg.T.astype(BF16), xw.astype(BF16))
        return 0

    lax.fori_loop(0, SSD_GROUPS, group, 0)
    xs_ref[0:base, :] = xs_ref[q:q + base, :]

    @pl.when(c == pl.num_programs(1) - 1)
    def _():
        for g in range(SSD_GROUPS):
            ssm_ref[g * gw:(g + 1) * gw, :] = ht_ref[:, g * gw:(g + 1) * gw].T


def ssd_prompt(z, xbc, dt_raw, conv_w, conv_b, dt_bias, a_log, d_exp, norm_w):
    nc = SEQ // SSD_CHUNK
    row_blk = lambda b, c: (b * nc + c, 0)
    const = lambda b, c: (0, 0)
    return pl.pallas_call(
        _ssd_prompt_body,
        grid=(BATCH, nc),
        in_specs=[pl.BlockSpec((SSD_CHUNK, SSD_D_INNER), row_blk),
                  pl.BlockSpec((SSD_CHUNK, SSD_CONV_DIM), row_blk),
                  pl.BlockSpec((SSD_CHUNK, SSD_HEADS), row_blk),
                  pl.BlockSpec((SSD_CONV, SSD_CONV_DIM), const),
                  pl.BlockSpec((1, SSD_CONV_DIM), const),
                  pl.BlockSpec((1, SSD_HEADS), const),
                  pl.BlockSpec((1, SSD_HEADS), const),
                  pl.BlockSpec((1, SSD_D_INNER), const),
                  pl.BlockSpec((1, SSD_D_INNER), const)],
        out_specs=[pl.BlockSpec((SSD_CHUNK, SSD_D_INNER), row_blk),
                   pl.BlockSpec((None, SSD_D_INNER, SSD_STATE), lambda b, c: (b, 0, 0))],
        out_shape=[jax.ShapeDtypeStruct((N_TOK, SSD_D_INNER), BF16),
                   jax.ShapeDtypeStruct((BATCH, SSD_D_INNER, SSD_STATE), F32)],
        scratch_shapes=[pltpu.VMEM((SSD_CARRY_ROWS + SSD_CHUNK, SSD_CONV_DIM), F32),
                        pltpu.VMEM((SSD_STATE, SSD_D_INNER), F32),
                        pltpu.VMEM((SSD_HEADS, SSD_CHUNK), F32),
                        pltpu.VMEM((SSD_HEADS, SSD_D_INNER), BF16),
                        pltpu.VMEM((SSD_HEADS, SSD_HEADS * SSD_CHUNK), BF16)],
        compiler_params=_cparams(("parallel", "arbitrary")),
        name="ssd_prompt",
    )(z, xbc, dt_raw, conv_w, conv_b.reshape(1, -1), dt_bias.reshape(1, -1), a_log.reshape(1, -1),
      d_exp, norm_w.reshape(1, -1))


def _ssd_sample_prep_body(x_ref, st_ref, cw_ref, cb_ref, dtr_ref, dtb_ref, alog_ref, dexp_ref,
                          conv_ref, st_out_ref, xdt_ref, xd_ref, dec_ref):
    j = pl.program_id(0)
    gw = SSD_GROUP_WIDTH
    x = x_ref[...]
    acc = cb_ref[...] + cw_ref[SSD_CONV - 1:SSD_CONV, :] * x
    for tap in range(SSD_CONV - 1):
        acc = acc + cw_ref[tap:tap + 1, :] * st_ref[tap]
    xc = _silu(acc)
    conv_ref[...] = xc
    for tap in range(SSD_CONV - 2):
        st_out_ref[tap] = st_ref[tap + 1]
    st_out_ref[SSD_CONV - 2] = x

    dt = _softplus(dtr_ref[...] + dtb_ref[...])
    dec_ref[...] = jnp.exp(dt * (-jnp.exp(alog_ref[...])))

    @pl.when(j < SSD_GROUPS)
    def _():
        head_row = lax.broadcasted_iota(jnp.int32, (SSD_HEADS, gw), 0)
        exp_lane = lax.broadcasted_iota(jnp.int32, (SSD_HEADS, gw), 1) // SSD_HEAD_DIM
        e_hp = jnp.where(head_row == j * SSD_HEADS_PER_GROUP + exp_lane, 1.0, 0.0).astype(BF16)
        xdt_ref[...] = xc * _dot3(dt, e_hp)
        xd_ref[...] = xc * dexp_ref[...]


def ssd_sample_prep(xbc, conv_state_t, dt_raw, conv_w, conv_b, dt_bias, a_log, d_exp):
    w = SSD_GROUP_WIDTH
    s_blk = N_PROMPT // DEC_BATCH
    col = lambda j: (0, j)
    xcol = lambda j: (0, jnp.minimum(j, SSD_GROUPS - 1))
    const = lambda j: (0, 0)
    return pl.pallas_call(
        _ssd_sample_prep_body,
        grid=(SSD_CONV_DIM // w,),
        in_specs=[pl.BlockSpec((DEC_BATCH, w), lambda j: (s_blk, j)),
                  pl.BlockSpec((SSD_CONV - 1, DEC_BATCH, w), lambda j: (0, 0, j)),
                  pl.BlockSpec((SSD_CONV, w), col),
                  pl.BlockSpec((1, w), col),
                  pl.BlockSpec((DEC_BATCH, SSD_HEADS), lambda j: (s_blk, 0)),
                  pl.BlockSpec((1, SSD_HEADS), const),
                  pl.BlockSpec((1, SSD_HEADS), const),
                  pl.BlockSpec((1, w), xcol)],
        out_specs=[pl.BlockSpec((DEC_BATCH, w), col),
                   pl.BlockSpec((SSD_CONV - 1, DEC_BATCH, w), lambda j: (0, 0, j)),
                   pl.BlockSpec((DEC_BATCH, w), xcol),
                   pl.BlockSpec((DEC_BATCH, w), xcol),
                   pl.BlockSpec((DEC_BATCH, SSD_HEADS), const)],
        out_shape=[jax.ShapeDtypeStruct((DEC_BATCH, SSD_CONV_DIM), F32),
                   jax.ShapeDtypeStruct((SSD_CONV - 1, DEC_BATCH, SSD_CONV_DIM), F32),
                   jax.ShapeDtypeStruct((DEC_BATCH, SSD_D_INNER), F32),
                   jax.ShapeDtypeStruct((DEC_BATCH, SSD_D_INNER), F32),
                   jax.ShapeDtypeStruct((DEC_BATCH, SSD_HEADS), F32)],
        compiler_params=_cparams(("arbitrary",)),
        name="ssd_sample_prep",
    )(xbc, conv_state_t, conv_w, conv_b.reshape(1, -1), dt_raw, dt_bias.reshape(1, -1), a_log.reshape(1, -1), d_exp)


SSD_STEP_ROWS = 2 * SSD_GROUPS


def _ssd_state_step_body(s_ref, xdt_ref, xd_ref, b_ref, c_ref, dec_ref, *rest):
    s_out_ref, y_ref = rest[-2:]
    hp = SSD_D_INNER
    row = lax.broadcasted_iota(jnp.int32, (SSD_STEP_ROWS, hp), 0)
    grp = lax.broadcasted_iota(jnp.int32, (SSD_STEP_ROWS, hp), 1) // SSD_GROUP_WIDTH
    own = row == grp
    m1 = jnp.where(own, jnp.broadcast_to(xdt_ref[...], (SSD_STEP_ROWS, hp)), 0.0).astype(BF16)
    pad = jnp.zeros((SSD_STEP_ROWS - SSD_GROUPS, SSD_STATE), F32)
    b16 = jnp.concatenate([b_ref[...], pad], axis=0).astype(BF16)
    c16 = jnp.concatenate([c_ref[...], pad], axis=0).astype(BF16)
    u = _dot_tn(m1, b16)
    dg = jnp.broadcast_to(dec_ref[...], (SSD_HEADS, SSD_HEADS)).T
    p_row = lax.broadcasted_iota(jnp.int32, (SSD_HEADS * V7X_SUBLANES, SSD_HEADS), 0) // V7X_SUBLANES
    p_lane = lax.broadcasted_iota(jnp.int32, (SSD_HEADS * V7X_SUBLANES, SSD_HEADS), 1)
    p8 = jnp.where(p_row == p_lane, 1.0, 0.0).astype(BF16)
    dcol8 = _dot3_left(p8, dg)
    blocks = SSD_HEAD_DIM // V7X_SUBLANES
    shape4 = (SSD_HEADS, blocks, V7X_SUBLANES, SSD_STATE)
    s_new = (s_ref[...].reshape(shape4) * dcol8.reshape(SSD_HEADS, 1, V7X_SUBLANES, SSD_STATE)
             + u.reshape(shape4)).reshape(hp, SSD_STATE)
    s_out_ref[...] = s_new
    y_all = _dot_nt(c16, s_new.astype(BF16))
    y_ref[...] = jnp.sum(jnp.where(own, y_all, 0.0), axis=0, keepdims=True) + xd_ref[...]


def ssd_state_step(state, layer, xdt, xd, b_g, c_g, dec, out_alias=None):
    per_b = lambda b: (b, 0, 0)
    n_ssd = state.shape[0]
    st_blk = pl.BlockSpec((None, None, SSD_D_INNER, SSD_STATE), lambda b: (layer, b, 0, 0))
    in_specs = [st_blk,
                pl.BlockSpec((None, 1, SSD_D_INNER), per_b),
                pl.BlockSpec((None, 1, SSD_D_INNER), per_b),
                pl.BlockSpec((None, SSD_GROUPS, SSD_STATE), per_b),
                pl.BlockSpec((None, SSD_GROUPS, SSD_STATE), per_b),
                pl.BlockSpec((None, 1, SSD_HEADS), per_b)]
    args = [state, xdt, xd, b_g, c_g, dec]
    aliases = {}
    if out_alias is not None:
        in_specs.append(pl.BlockSpec(memory_space=pl.ANY))
        args.append(out_alias)
        aliases = {len(args) - 1: 0}
    return pl.pallas_call(
        _ssd_state_step_body,
        grid=(DEC_BATCH,),
        in_specs=in_specs,
        out_specs=[st_blk, pl.BlockSpec((None, 1, SSD_D_INNER), per_b)],
        out_shape=[jax.ShapeDtypeStruct((n_ssd, DEC_BATCH, SSD_D_INNER, SSD_STATE), F32),
                   jax.ShapeDtypeStruct((DEC_BATCH, 1, SSD_D_INNER), F32)],
        input_output_aliases=aliases,
        compiler_params=_cparams(("parallel",)),
        name="ssd_state_step",
    )(*args)


def _gated_norm_body(y_ref, z_ref, w_ref, mix_ref, o_ref):
    del mix_ref
    gw = SSD_GROUP_WIDTH
    for g in range(SSD_GROUPS):
        sl = slice(g * gw, (g + 1) * gw)
        gt = y_ref[:, sl] * _silu(z_ref[:, sl])
        gt = gt * lax.rsqrt(jnp.mean(gt * gt, axis=1, keepdims=True) + EPS) * w_ref[:, sl]
        o_ref[:, sl] = gt.astype(o_ref.dtype)


def gated_norm_sample(y_s, z, norm_w, mix):
    s_blk = N_PROMPT // DEC_BATCH
    return pl.pallas_call(
        _gated_norm_body,
        grid=(1,),
        in_specs=[pl.BlockSpec((DEC_BATCH, SSD_D_INNER), lambda i: (0, 0)),
                  pl.BlockSpec((DEC_BATCH, SSD_D_INNER), lambda i: (s_blk, 0)),
                  pl.BlockSpec((1, SSD_D_INNER), lambda i: (0, 0)),
                  pl.BlockSpec(memory_space=pl.ANY)],
        out_specs=pl.BlockSpec((DEC_BATCH, SSD_D_INNER), lambda i: (s_blk, 0)),
        out_shape=jax.ShapeDtypeStruct(mix.shape, mix.dtype),
        input_output_aliases={3: 0},
        compiler_params=_cparams(("arbitrary",)),
        name="gated_norm_sample",
    )(y_s, z, norm_w.reshape(1, -1), mix)


def _pack_weights(w_mem_kv, ssd_w_in, ssd_w_out, mla_w_in, mla_w_uq, mla_w_ukv, mla_w_out):
    ssd_in = ssd_w_in.astype(BF16)
    ssd_x = ssd_w_in[:, :, SSD_MIX_COLS:].astype(BF16)
    o_pe = MLA_Q_LORA + MLA_KV_LORA
    o_g = o_pe + MLA_ROPE
    mla_a = jnp.concatenate([mla_w_in[:, :, :o_g], mla_w_in[:, :, o_pe:o_g]], axis=2).astype(BF16)
    mla_rest = mla_w_in[:, :, o_g:].astype(BF16)
    uq4 = mla_w_uq.reshape(mla_w_uq.shape[0], MLA_Q_LORA, MLA_HEADS, MLA_NOPE + MLA_ROPE)
    uq = jnp.concatenate([uq4[..., :MLA_NOPE].reshape(-1, MLA_Q_LORA, MLA_HEADS * MLA_NOPE),
                          uq4[..., MLA_NOPE:].reshape(-1, MLA_Q_LORA, MLA_HEADS * MLA_ROPE)], axis=2).astype(BF16)
    return dict(mem_kv=w_mem_kv.astype(BF16), ssd_in=ssd_in, ssd_x=ssd_x, ssd_out=ssd_w_out.astype(BF16), mla_a=mla_a, mla_rest=mla_rest,
                uq=uq, ukv=mla_w_ukv.astype(BF16), mla_out=mla_w_out.astype(BF16))


def _memory_branch(u, mem_n, w, w_in, layer, j, off_xq, off_xg, cache_mem_k, cache_mem_v):
    mk = matmul(mem_n, w["mem_kv"], layer, X_WIDTH, F32)
    mv = matmul(mem_n, w["mem_kv"], layer, X_WIDTH, F32, col0=X_WIDTH)
    xq = matmul(u, w_in, j, X_WIDTH, BF16, col0=off_xq)
    xg = matmul(u, w_in, j, X_WIDTH, F32, col0=off_xg)
    mem_p = cross_attend_prompt(xq, xg, mk, mv)
    mem_s = cross_attend_sample(xq[N_PROMPT:].astype(F32).reshape(DEC_BATCH, X_HEADS, X_HEAD_DIM),
                                xg[N_PROMPT:].reshape(DEC_BATCH, X_HEADS, X_HEAD_DIM),
                                cache_mem_k, cache_mem_v, layer)
    mem = fill_sample_rows(mem_p, mem_s.reshape(DEC_BATCH, X_WIDTH))
    return mem, mk, mv


def kernel(x_prompt, x_sample, cache_ckv, cache_kpe, state_ssm, state_conv, cache_mem_k, cache_mem_v, page_table, mem_prompt, norm_w, final_norm_w, mem_norm_w, w_mem_kv, ssd_w_in, ssd_conv_w, ssd_conv_b, ssd_dt_bias, ssd_A_log, ssd_D, ssd_norm_w, ssd_w_out, mla_w_in, mla_q_norm_w, mla_kv_norm_w, mla_w_uq, mla_w_ukv, mla_w_out):
    h = jnp.concatenate([x_prompt.reshape(N_PROMPT, D_MODEL), x_sample.reshape(DEC_BATCH, D_MODEL)], axis=0)
    mem2d = mem_prompt.reshape(BATCH * N_MEM, D_MODEL)
    state4 = state_ssm.reshape(state_ssm.shape[0], DEC_BATCH, SSD_D_INNER, SSD_STATE)
    cache_kpe_t = jnp.swapaxes(cache_kpe, 2, 3)
    cos2, sin2 = _rope_tables()
    w = _pack_weights(w_mem_kv, ssd_w_in, ssd_w_out, mla_w_in, mla_w_uq, mla_w_ukv, mla_w_out)

    ckv_p, kpe_p, ckv_s, kpe_s = [], [], [], []
    ssm_p, conv_p, conv_s, mk_out, mv_out = [], [], [], [], []
    ssm_s_all = None

    for i in range(DEPTH):
        j = i // 2
        mem_n, = rmsnorm(mem2d, mem_norm_w[i], [BF16])
        u, = rmsnorm(h, norm_w[i], [BF16])
        if i % 2 == 0:
            w_in = w["ssd_in"]
            z = matmul(u, w_in, j, SSD_D_INNER, F32, col0=SSD_IN_OFF_Z)
            xbc = matmul(u, w_in, j, SSD_CONV_DIM, F32, col0=SSD_IN_OFF_XBC)
            dt_raw = matmul(u, w_in, j, SSD_HEADS, F32, col0=SSD_IN_OFF_DT)
            mem, mk, mv = _memory_branch(u, mem_n, w, w["ssd_x"], i, j, 0, X_WIDTH, cache_mem_k, cache_mem_v)

            d_exp = jnp.repeat(ssd_D[j], SSD_HEAD_DIM).reshape(1, SSD_D_INNER)
            mix, ssm_pi = ssd_prompt(z, xbc, dt_raw, ssd_conv_w[j], ssd_conv_b[j], ssd_dt_bias[j], ssd_A_log[j],
                                     d_exp, ssd_norm_w[j])
            ssm_p.append(ssm_pi.reshape(BATCH, SSD_HEADS, SSD_HEAD_DIM, SSD_STATE))
            conv_p.append(jnp.stack([xbc[(b + 1) * SEQ - (SSD_CONV - 1):(b + 1) * SEQ] for b in range(BATCH)]))

            xconv, conv_new_t, xdt, xd, dec = ssd_sample_prep(
                xbc, jnp.transpose(state_conv[j], (1, 0, 2)), dt_raw, ssd_conv_w[j], ssd_conv_b[j],
                ssd_dt_bias[j], ssd_A_log[j], d_exp)
            conv_s.append(jnp.transpose(conv_new_t, (1, 0, 2)))
            gn = SSD_GROUPS * SSD_STATE
            b_g = xconv[:, SSD_D_INNER:SSD_D_INNER + gn].reshape(DEC_BATCH, SSD_GROUPS, SSD_STATE)
            c_g = xconv[:, SSD_D_INNER + gn:].reshape(DEC_BATCH, SSD_GROUPS, SSD_STATE)
            ssm_s_all, y_s = ssd_state_step(state4, j, xdt.reshape(DEC_BATCH, 1, SSD_D_INNER),
                                            xd.reshape(DEC_BATCH, 1, SSD_D_INNER), b_g, c_g,
                                            dec.reshape(DEC_BATCH, 1, SSD_HEADS), out_alias=ssm_s_all)
            mix = gated_norm_sample(y_s.reshape(DEC_BATCH, SSD_D_INNER), z, ssd_norm_w[j], mix)
            w_out = w["ssd_out"]
        else:
            w_in = w["mla_rest"]
            a = matmul(u, w["mla_a"], j, MLA_A_COLS, F32)
            gate = matmul(u, w_in, j, MLA_WIDTH, F32, col0=MLA_REST_OFF_GATE)
            mem, mk, mv = _memory_branch(u, mem_n, w, w_in, i, j, MLA_REST_OFF_XQ, MLA_REST_OFF_XG, cache_mem_k, cache_mem_v)

            cqn, = rmsnorm(a, mla_q_norm_w[j], [BF16], width=MLA_Q_LORA, col_block=0)
            ckv, ckv_b, kpe, kpe_dup = mla_prep(a, mla_kv_norm_w[j], cos2, sin2)
            ckv_p.append(ckv[:N_PROMPT].reshape(BATCH, SEQ, MLA_KV_LORA))
            kpe_p.append(kpe[:N_PROMPT].reshape(BATCH, SEQ, MLA_ROPE))
            ckv_s.append(ckv[N_PROMPT:].reshape(DEC_BATCH, 1, MLA_KV_LORA))
            kpe_s.append(kpe[N_PROMPT:].reshape(DEC_BATCH, 1, MLA_ROPE))

            qn = matmul(cqn, w["uq"], j, MLA_HEADS * MLA_NOPE, BF16, out_scale=MLA_SCALE_LOG2)
            qp_raw = matmul(cqn, w["uq"], j, MLA_HEADS * MLA_ROPE, F32, col0=MLA_HEADS * MLA_NOPE)
            qp = rope_heads(qp_raw, cos2, sin2, BF16, MLA_SCALE_LOG2)
            kvup = matmul(ckv_b, w["ukv"], j, MLA_HEADS * (MLA_NOPE + MLA_V), BF16)
            mix = mla_flash_prompt(qn, qp, kvup, kpe_dup, gate)

            q_lat = mla_absorb_q(qn[N_PROMPT:], w["ukv"], j).reshape(DEC_BATCH, MLA_HEADS, MLA_KV_LORA)
            o_lat = mla_paged_sample(q_lat, qp[N_PROMPT:].reshape(DEC_BATCH, MLA_HEADS, MLA_ROPE),
                                     ckv_s[-1], kpe_s[-1], cache_ckv, cache_kpe_t, page_table, j)
            mix = mla_merge_sample(o_lat.reshape(DEC_BATCH, MLA_HEADS * MLA_KV_LORA), w["ukv"], j, gate, mix)
            w_out = w["mla_out"]

        mk_out.append(mk.reshape(BATCH, N_MEM, X_HEADS, X_HEAD_DIM))
        mv_out.append(mv.reshape(BATCH, N_MEM, X_HEADS, X_HEAD_DIM))
        h = matmul([mix, mem], w_out, j, D_MODEL, F32, res=h)

    y_prompt, = rmsnorm(h, final_norm_w, [F32], rows=N_PROMPT)
    y_sample, = rmsnorm(h, final_norm_w, [F32], row_start=N_PROMPT, rows=DEC_BATCH)
    ssm_sample = ssm_s_all.reshape(ssm_s_all.shape[0], DEC_BATCH, SSD_HEADS, SSD_HEAD_DIM, SSD_STATE)
    return (y_prompt.reshape(BATCH, SEQ, D_MODEL), y_sample.reshape(DEC_BATCH, 1, D_MODEL),
            jnp.stack(ckv_p), jnp.stack(kpe_p), jnp.stack(ckv_s), jnp.stack(kpe_s),
            jnp.stack(ssm_p), ssm_sample, jnp.stack(conv_p), jnp.stack(conv_s),
            jnp.stack(mk_out), jnp.stack(mv_out))
```
